```python
import math
import jax
import jax.numpy as jnp
from jax import lax
import numpy as np

D_MODEL = 1024
BATCH = 2
SEQ = 16384
DEPTH = 2
DEC_BATCH = 16
DEC_SEQ = 2048
PAST_LEN = 128

GRID_W = 64
EPS = 1e-6
ROPE_THETA = 500000.0
ROPE_FRACTION = 4
N_BRANCH = 4
BRANCH_W = D_MODEL // N_BRANCH
N_IN_SEGMENTS = 11
DA_HEADS = 4
DA_DIM = BRANCH_W // (2 * DA_HEADS)
DA_VDIM = 2 * DA_DIM
Q_BLOCK = 128
NA_HEADS = 4
NA_DIM = BRANCH_W // NA_HEADS
NA_ROWS = 8
NA_COLS = 16
S5_W = BRANCH_W
S5_GROUP = 16
S5_GROUPS = S5_W // S5_GROUP
S5_STATE = 64
S5_DT_MIN = 0.001
S5_DT_MAX = 0.1
RET_HEADS = 4
RET_DK = BRANCH_W // RET_HEADS
RET_DV = BRANCH_W // RET_HEADS
RET_CHUNK = 128
D_FF = 2816
CONV_WIDTH = 3

kernel_name = 'hybrid_bidir_encoder_dual_batch'


def rms_norm(x, g):
    xf = x.astype(jnp.float32)
    y = xf * lax.rsqrt(jnp.mean(xf * xf, axis=-1, keepdims=True) + EPS)
    return (y * g.astype(jnp.float32)).astype(x.dtype)


def split_heads(t, n_heads):
    B, L, _ = t.shape
    return t.reshape(B, L, n_heads, -1).transpose(0, 2, 1, 3)


def merge_heads(t):
    B, H, L, d = t.shape
    return t.transpose(0, 2, 1, 3).reshape(B, L, H * d)


def partial_rotary(t, pos):
    rot = t.shape[-1] // ROPE_FRACTION
    half = rot // 2
    inv_freq = jnp.power(jnp.float32(ROPE_THETA), -jnp.arange(half, dtype=jnp.float32) / half)
    ang = pos.astype(jnp.float32)[:, None] * inv_freq[None, :]
    cos = jnp.cos(ang).astype(t.dtype)
    sin = jnp.sin(ang).astype(t.dtype)
    t1, t2, rest = t[..., :half], t[..., half:rot], t[..., rot:]
    return jnp.concatenate([t1 * cos - t2 * sin, t2 * cos + t1 * sin, rest], axis=-1)


def diff_attention(q, k, v, lam):
    B, H, _, L, d = q.shape
    nb = L // Q_BLOCK
    qb = q.reshape(B, H, 2, nb, Q_BLOCK, d).transpose(3, 0, 1, 2, 4, 5)
    scale = d ** -0.5

    def block(q_i):
        s = jnp.einsum('bhcqd,bhckd->bhcqk', q_i, k).astype(jnp.float32) * scale
        p = jax.nn.softmax(s, axis=-1)
        a = p[:, :, 0] - lam * p[:, :, 1]
        return jnp.einsum('bhqk,bhkv->bhqv', a.astype(v.dtype), v)

    o = lax.map(block, qb)
    return o.transpose(1, 2, 0, 3, 4).reshape(B, H, L, v.shape[-1])


def neighbourhood_attention(q, k, v, rpb):
    B, H, L, d = q.shape
    rows = L // GRID_W
    kr = min(NA_ROWS, rows)
    kg = k.reshape(B, H, rows, GRID_W, d)
    vg = v.reshape(B, H, rows, GRID_W, d)
    qs = q.reshape(B, H, rows, GRID_W, d).transpose(2, 0, 1, 3, 4)
    col = jnp.arange(GRID_W)
    col_idx = jnp.clip(col - NA_COLS // 2, 0, GRID_W - NA_COLS)[:, None] + jnp.arange(NA_COLS)[None, :]
    dcol = col_idx - col[:, None] + (NA_COLS - 1)
    row_ids = jnp.arange(rows)
    row_start = jnp.clip(row_ids - kr // 2, 0, rows - kr)
    scale = d ** -0.5
    rpb32 = rpb.astype(jnp.float32)

    def one_row(args):
        r, rs, q_r = args
        k_win = lax.dynamic_slice_in_dim(kg, rs, kr, axis=2)[:, :, :, col_idx, :]
        v_win = lax.dynamic_slice_in_dim(vg, rs, kr, axis=2)[:, :, :, col_idx, :]
        drow = rs + jnp.arange(kr) - r + (NA_ROWS - 1)
        bias = rpb32[:, drow[None, :, None], dcol[:, None, :]]
        s = jnp.einsum('bhwd,bhrwcd->bhwrc', q_r, k_win).astype(jnp.float32) * scale + bias[None]
        p = jax.nn.softmax(s.reshape(B, H, GRID_W, kr * NA_COLS), axis=-1).reshape(s.shape)
        return jnp.einsum('bhwrc,bhrwcd->bhwd', p.astype(v_win.dtype), v_win)

    o = lax.map(one_row, (row_ids, row_start, qs))
    return o.transpose(1, 2, 0, 3, 4).reshape(B, H, L, d)


def _complex_affine_combine(e1, e2):
    a1r, a1i, b1r, b1i = e1
    a2r, a2i, b2r, b2i = e2
    return (a2r * a1r - a2i * a1i,
            a2r * a1i + a2i * a1r,
            a2r * b1r - a2i * b1i + b2r,
            a2r * b1i + a2i * b1r + b2i)


def s5_scan(u, a_re, a_im, log_dt, b_re, b_im, c_re, c_im, reverse):
    f32 = jnp.float32
    lam_re = jnp.minimum(a_re.astype(f32), -1e-4)
    lam_im = a_im.astype(f32)
    dt = jnp.exp(log_dt.astype(f32))[:, None]
    mag = jnp.exp(lam_re * dt)
    abar_re = mag * jnp.cos(lam_im * dt)
    abar_im = mag * jnp.sin(lam_im * dt)
    den = lam_re * lam_re + lam_im * lam_im
    f_re = ((abar_re - 1.0) * lam_re + abar_im * lam_im) / den
    f_im = (abar_im * lam_re - (abar_re - 1.0) * lam_im) / den
    br = b_re.astype(f32)
    bi = b_im.astype(f32)
    bb_re = f_re[..., None] * br - f_im[..., None] * bi
    bb_im = f_re[..., None] * bi + f_im[..., None] * br
    bu_re = jnp.einsum('gph,blgh->blgp', bb_re, u)
    bu_im = jnp.einsum('gph,blgh->blgp', bb_im, u)
    ar = jnp.broadcast_to(abar_re, bu_re.shape)
    ai = jnp.broadcast_to(abar_im, bu_re.shape)
    _, _, s_re, s_im = lax.associative_scan(_complex_affine_combine, (ar, ai, bu_re, bu_im), reverse=reverse, axis=1)
    return (jnp.einsum('ghp,blgp->blgh', c_re.astype(f32), s_re)
            - jnp.einsum('ghp,blgp->blgh', c_im.astype(f32), s_im))


def s5_layer(su, a_re, a_im, log_dt, b_re, b_im, c_re, c_im, d, glu_w, glu_b):
    B, L, _ = su.shape
    uf = su.astype(jnp.float32)
    u = uf.reshape(B, L, S5_GROUPS, S5_GROUP)
    y = d.astype(jnp.float32) * uf
    for direction in range(2):
        y = y + s5_scan(u, a_re[direction], a_im[direction], log_dt[direction], b_re[direction],
                        b_im[direction], c_re[direction], c_im[direction],
                        reverse=(direction == 1)).reshape(B, L, S5_W)
    y = jax.nn.gelu(y)
    y = y * jax.nn.sigmoid(y @ glu_w.astype(jnp.float32) + glu_b.astype(jnp.float32))
    return y.astype(su.dtype)


def causal_retention(q, k, v, log_gamma, include_diag):
    B, H, L, dk = q.shape
    dv = v.shape[-1]
    C = RET_CHUNK
    n = L // C
    qc = q.reshape(B, H, n, C, dk)
    kc = k.reshape(B, H, n, C, dk)
    vc = v.reshape(B, H, n, C, dv)
    idx = jnp.arange(C, dtype=jnp.float32)
    lg = log_gamma.astype(jnp.float32)
    rel = idx[:, None] - idx[None, :]
    mask = (rel >= 0) if include_diag else (rel > 0)
    decay_in = jnp.where(mask[None], jnp.exp(lg[:, None, None] * jnp.where(mask, rel, 0.0)[None]), 0.0)
    s = jnp.einsum('bhnid,bhnjd->bhnij', qc, kc) * decay_in[None, :, None].astype(q.dtype)
    inner = jnp.einsum('bhnij,bhnje->bhnie', s, vc)
    zeta = jnp.exp(lg[:, None] * (C - 1.0 - idx)[None]).astype(q.dtype)
    kv = jnp.einsum('bhnjd,hj,bhnje->nbhde', kc, zeta, vc)
    chunk_decay = jnp.exp(lg * C).astype(q.dtype)[None, :, None, None]

    def step(state, kv_n):
        return state * chunk_decay + kv_n, state

    _, prev = lax.scan(step, jnp.zeros_like(kv[0]), kv)
    xi = jnp.exp(lg[:, None] * (idx + 1.0)[None]).astype(q.dtype)
    cross = jnp.einsum('bhnid,nbhde->bhnie', qc, prev) * xi[None, :, None, :, None]
    return (inner + cross).reshape(B, H, L, dv)


def retention_layer(rq, rk, rv, rg, ret_log_decay, ret_norm_g):
    q = split_heads(rq, RET_HEADS)
    k = split_heads(rk, RET_HEADS) * (RET_DK ** -0.5)
    v = split_heads(rv, RET_HEADS)
    log_gamma = -jnp.exp(ret_log_decay.astype(jnp.float32))
    fwd = causal_retention(q, k, v, log_gamma[0], True)
    bwd = jnp.flip(causal_retention(jnp.flip(q, 2), jnp.flip(k, 2), jnp.flip(v, 2), log_gamma[1], False), 2)
    o = rms_norm(fwd + bwd, ret_norm_g)
    return merge_heads(o) * jax.nn.silu(rg)


def token_mixer(h, lam_init, w_in, da_lambda, da_subln_g, na_rpb, s5_a_re, s5_a_im, s5_log_dt,
                s5_b_re, s5_b_im, s5_c_re, s5_c_im, s5_d, s5_glu_w, s5_glu_b, ret_log_decay,
                ret_norm_g, w_branch, w_branch_gate, w_out):
    B, L, _ = h.shape
    pos = jnp.arange(L)
    aq, ak, av, nq, nk, nv, su, rq, rk, rv, rg = jnp.split(h @ w_in, N_IN_SEGMENTS, axis=-1)

    def da_heads(t):
        return t.reshape(B, L, DA_HEADS, 2, DA_DIM).transpose(0, 2, 3, 1, 4)
    qa = partial_rotary(da_heads(aq), pos)
    ka = partial_rotary(da_heads(ak), pos)
    lam_f = da_lambda.astype(jnp.float32)
    lam = jnp.exp(jnp.sum(lam_f[0] * lam_f[1])) - jnp.exp(jnp.sum(lam_f[2] * lam_f[3])) + lam_init
    o_a = diff_attention(qa, ka, split_heads(av, DA_HEADS), lam)
    o_a = merge_heads(rms_norm(o_a, da_subln_g) * (1.0 - lam_init))

    o_b = merge_heads(neighbourhood_attention(split_heads(nq, NA_HEADS), split_heads(nk, NA_HEADS),
                                              split_heads(nv, NA_HEADS), na_rpb))

    o_c = s5_layer(su, s5_a_re, s5_a_im, s5_log_dt, s5_b_re, s5_b_im, s5_c_re, s5_c_im,
                   s5_d, s5_glu_w, s5_glu_b)

    o_d = retention_layer(rq, rk, rv, rg, ret_log_decay, ret_norm_g)

    branches = (o_a, o_b, o_c, o_d)
    merged = jax.nn.sigmoid(h @ w_branch_gate[0]) * (branches[0] @ w_branch[0])
    for i in range(1, N_BRANCH):
        merged = merged + jax.nn.sigmoid(h @ w_branch_gate[i]) * (branches[i] @ w_branch[i])
    return merged @ w_out


def conv_ffn(h, w_up, w_gate, conv_w, conv_b, w_down):
    a = h @ w_up
    a = lax.conv_general_dilated(a, conv_w.astype(a.dtype), window_strides=(1,),
                                 padding=[(CONV_WIDTH // 2, CONV_WIDTH // 2)],
                                 dimension_numbers=('NWC', 'WIO', 'NWC'),
                                 feature_group_count=a.shape[-1]) + conv_b
    return (jax.nn.gelu(a) * (h @ w_gate)) @ w_down


def layer(x, c, lam_init, norm1_g, norm2_g, ada_w, ada_b, w_in, da_lambda, da_subln_g, na_rpb,
          s5_a_re, s5_a_im, s5_log_dt, s5_b_re, s5_b_im, s5_c_re, s5_c_im, s5_d, s5_glu_w,
          s5_glu_b, ret_log_decay, ret_norm_g, w_branch, w_branch_gate, w_out, ffn_w_up,
          ffn_w_gate, ffn_conv_w, ffn_conv_b, ffn_w_down):
    mod = jax.nn.silu(c) @ ada_w + ada_b
    sh1, sc1, g1, sh2, sc2, g2 = [m[:, None, :] for m in jnp.split(mod, 6, axis=-1)]
    h = rms_norm(x, norm1_g) * (1.0 + sc1) + sh1
    x = x + g1 * token_mixer(h, lam_init, w_in, da_lambda, da_subln_g, na_rpb, s5_a_re, s5_a_im,
                             s5_log_dt, s5_b_re, s5_b_im, s5_c_re, s5_c_im, s5_d, s5_glu_w,
                             s5_glu_b, ret_log_decay, ret_norm_g, w_branch, w_branch_gate, w_out)
    h = rms_norm(x, norm2_g) * (1.0 + sc2) + sh2
    x = x + g2 * conv_ffn(h, ffn_w_up, ffn_w_gate, ffn_conv_w, ffn_conv_b, ffn_w_down)
    return x


def trunk(x, c, norm1_g, norm2_g, ada_w, ada_b, w_in, da_lambda, da_subln_g, na_rpb,
          s5_a_re, s5_a_im, s5_log_dt, s5_b_re, s5_b_im, s5_c_re, s5_c_im, s5_d, s5_glu_w,
          s5_glu_b, ret_log_decay, ret_norm_g, w_branch, w_branch_gate, w_out, ffn_w_up,
          ffn_w_gate, ffn_conv_w, ffn_conv_b, ffn_w_down, final_norm_g):
    for l in range(DEPTH):
        lam_init = 0.8 - 0.6 * math.exp(-0.3 * l)
        x = layer(x, c, lam_init, norm1_g[l], norm2_g[l], ada_w[l], ada_b[l], w_in[l], da_lambda[l],
                  da_subln_g[l], na_rpb[l], s5_a_re[l], s5_a_im[l], s5_log_dt[l], s5_b_re[l],
                  s5_b_im[l], s5_c_re[l], s5_c_im[l], s5_d[l], s5_glu_w[l], s5_glu_b[l],
                  ret_log_decay[l], ret_norm_g[l], w_branch[l], w_branch_gate[l], w_out[l],
                  ffn_w_up[l], ffn_w_gate[l], ffn_conv_w[l], ffn_conv_b[l], ffn_w_down[l])
    return rms_norm(x, final_norm_g)


def setup_inputs(seed: int = 0) -> dict:
    key = jax.random.key(seed)
    keys = iter(jax.random.split(key, 48))

    def nrm(shape, std):
        return jax.random.normal(next(keys), shape, jnp.float32) * std

    D = D_MODEL
    ret_init = jnp.log(-jnp.log1p(-jnp.power(2.0, -5.0 - jnp.arange(RET_HEADS, dtype=jnp.float32))))
    a_im_init = math.pi * jnp.arange(S5_STATE, dtype=jnp.float32)
    return {
        'x_prompt': nrm((BATCH, SEQ, D), 1.0),
        'x_sample': nrm((DEC_BATCH, DEC_SEQ, D), 1.0),
        'c_prompt': nrm((BATCH, D), 1.0),
        'c_sample': nrm((DEC_BATCH, D), 1.0),
        'norm1_g': 1.0 + nrm((DEPTH, D), 0.02),
        'norm2_g': 1.0 + nrm((DEPTH, D), 0.02),
        'ada_w': nrm((DEPTH, D, 6 * D), 0.5 * D ** -0.5),
        'ada_b': nrm((DEPTH, 6 * D), 0.02),
        'w_in': nrm((DEPTH, D, N_IN_SEGMENTS * BRANCH_W), D ** -0.5),
        'da_lambda': nrm((DEPTH, 4, DA_DIM), 0.1),
        'da_subln_g': 1.0 + nrm((DEPTH, DA_VDIM), 0.02),
        'na_rpb': nrm((DEPTH, NA_HEADS, 2 * NA_ROWS - 1, 2 * NA_COLS - 1), 0.02),
        's5_a_re': -0.5 + nrm((DEPTH, 2, S5_GROUPS, S5_STATE), 0.01),
        's5_a_im': a_im_init + nrm((DEPTH, 2, S5_GROUPS, S5_STATE), 0.01),
        's5_log_dt': jax.random.uniform(next(keys), (DEPTH, 2, S5_GROUPS), jnp.float32,
                                        math.log(S5_DT_MIN), math.log(S5_DT_MAX)),
        's5_b_re': nrm((DEPTH, 2, S5_GROUPS, S5_STATE, S5_GROUP), (2 * S5_GROUP) ** -0.5),
        's5_b_im': nrm((DEPTH, 2, S5_GROUPS, S5_STATE, S5_GROUP), (2 * S5_GROUP) ** -0.5),
        's5_c_re': nrm((DEPTH, 2, S5_GROUPS, S5_GROUP, S5_STATE), (2 * S5_STATE) ** -0.5),
        's5_c_im': nrm((DEPTH, 2, S5_GROUPS, S5_GROUP, S5_STATE), (2 * S5_STATE) ** -0.5),
        's5_d': nrm((DEPTH, S5_W), 0.5),
        's5_glu_w': nrm((DEPTH, S5_W, S5_W), S5_W ** -0.5),
        's5_glu_b': nrm((DEPTH, S5_W), 0.02),
        'ret_log_decay': ret_init[None, None, :] + nrm((DEPTH, 2, RET_HEADS), 0.05),
        'ret_norm_g': 1.0 + nrm((DEPTH, RET_DV), 0.02),
        'w_branch': nrm((DEPTH, N_BRANCH, BRANCH_W, D), BRANCH_W ** -0.5),
        'w_branch_gate': nrm((DEPTH, N_BRANCH, D, D), D ** -0.5),
        'w_out': nrm((DEPTH, D, D), D ** -0.5),
        'ffn_w_up': nrm((DEPTH, D, D_FF), D ** -0.5),
        'ffn_w_gate': nrm((DEPTH, D, D_FF), D ** -0.5),
        'ffn_conv_w': nrm((DEPTH, CONV_WIDTH, 1, D_FF), CONV_WIDTH ** -0.5),
        'ffn_conv_b': nrm((DEPTH, D_FF), 0.02),
        'ffn_w_down': nrm((DEPTH, D_FF, D), D_FF ** -0.5),
        'final_norm_g': 1.0 + nrm((D,), 0.02),
    }


def reference(x_prompt, x_sample, c_prompt, c_sample, norm1_g, norm2_g, ada_w, ada_b, w_in,
              da_lambda, da_subln_g, na_rpb, s5_a_re, s5_a_im, s5_log_dt, s5_b_re, s5_b_im,
              s5_c_re, s5_c_im, s5_d, s5_glu_w, s5_glu_b, ret_log_decay, ret_norm_g, w_branch,
              w_branch_gate, w_out, ffn_w_up, ffn_w_gate, ffn_conv_w, ffn_conv_b, ffn_w_down,
              final_norm_g):
    weights = (norm1_g, norm2_g, ada_w, ada_b, w_in, da_lambda, da_subln_g, na_rpb, s5_a_re,
               s5_a_im, s5_log_dt, s5_b_re, s5_b_im, s5_c_re, s5_c_im, s5_d, s5_glu_w, s5_glu_b,
               ret_log_decay, ret_norm_g, w_branch, w_branch_gate, w_out, ffn_w_up, ffn_w_gate,
               ffn_conv_w, ffn_conv_b, ffn_w_down, final_norm_g)
    y_prompt = trunk(x_prompt, c_prompt, *weights)
    y_sample = trunk(x_sample, c_sample, *weights)
    return (y_prompt, y_sample)
```

```python
import functools
import math

import numpy as np
import jax
import jax.numpy as jnp
from jax import lax
from jax.experimental import pallas as pl
from jax.experimental.pallas import tpu as pltpu

F32 = jnp.float32
BF16 = jnp.bfloat16

EPS = 1e-6
ROPE_THETA = 500000.0
BRANCH_W = 256
N_SEG = 11
DA_DIM = 32
HEAD_W = 64
N_HEADS = 4
GRID_W = 64
NA_ROWS = 8
NA_COLS = 16
S5_G = 16
S5_P = 64
S5_H = 16
S5_NSEQ = 16
D_FF = 2816
NEG = -1e30
LOG2E = 1.4426950408889634

VMEM_LIMIT = 56 * 1024 * 1024


def _cparams(sem):
    return pltpu.CompilerParams(dimension_semantics=sem, vmem_limit_bytes=VMEM_LIMIT)


def _const_spec(shape):
    nd = len(shape)
    return pl.BlockSpec(shape, lambda *_: (0,) * nd)


def _norm_mod(x, g, sc, sh):
    ms = jnp.mean(x * x, axis=-1, keepdims=True)
    y = x * lax.rsqrt(ms + EPS) * g
    return y * (1.0 + sc) + sh


def _mod_kernel(c_ref, w_ref, b_ref, o_ref):
    c = c_ref[...]
    s = c * jax.nn.sigmoid(c)
    o_ref[0] = jnp.dot(s.astype(BF16), w_ref[0].astype(BF16),
                       preferred_element_type=F32) + b_ref[0]


def _mod_call(c_all, ada_w, ada_b):
    depth, d, n6 = ada_w.shape
    nb = c_all.shape[0]
    tn = 1536
    return pl.pallas_call(
        _mod_kernel,
        grid=(depth, n6 // tn),
        in_specs=[pl.BlockSpec((nb, d), lambda l, j: (0, 0)),
                  pl.BlockSpec((1, d, tn), lambda l, j: (l, 0, j)),
                  pl.BlockSpec((1, 1, tn), lambda l, j: (l, 0, j))],
        out_specs=pl.BlockSpec((1, nb, tn), lambda l, j: (l, 0, j)),
        out_shape=jax.ShapeDtypeStruct((depth, nb, n6), F32),
        compiler_params=_cparams(("arbitrary", "arbitrary")),
        name="adaln_mod",
    )(c_all, ada_w, ada_b.reshape(depth, 1, n6))


class _Geom:
    def __init__(self, b, l):
        assert S5_NSEQ % b == 0
        self.b, self.l = b, l
        self.spb = S5_NSEQ // b
        assert l % self.spb == 0
        self.lseg = l // self.spb
        assert self.lseg & (self.lseg - 1) == 0
        self.tm = min(512, self.lseg)
        assert self.lseg % self.tm == 0
        self.tps = self.lseg // self.tm
        self.nt = l // self.tm
        self.rows = l // GRID_W
        assert self.rows >= NA_ROWS and l % (NA_ROWS * GRID_W) == 0

    def scan_index(self, b, i):
        n = b * self.spb + i // self.tps
        return n // 8, i % self.tps, n % 8


def _proj_kernel(x_ref, g_ref, sc_ref, sh_ref, w_ref, cos_ref, sina_ref, sinb_ref,
                 qT_ref, k_ref, vT_ref, nq_ref, nk_ref, nv_ref, su_ref,
                 rq_ref, rk_ref, rv_ref, rg_ref):
    h = _norm_mod(x_ref[0], g_ref[...], sc_ref[0], sh_ref[0]).astype(BF16)

    def seg(j):
        return jnp.dot(h, w_ref[:, j * BRANCH_W:(j + 1) * BRANCH_W],
                       preferred_element_type=F32)

    cos, sina, sinb = cos_ref[...], sina_ref[...], sinb_ref[...]

    def rot(p):
        halves = []
        for s in range(2):
            ph = p[:, s * 128:(s + 1) * 128]
            halves.append(ph * cos + pltpu.roll(ph, 124, 1) * sina + pltpu.roll(ph, 4, 1) * sinb)
        return jnp.concatenate(halves, axis=1)

    qscale = DA_DIM ** -0.5 * LOG2E
    qT_ref[0] = (rot(seg(0)) * qscale).T.astype(BF16)
    k_ref[0] = rot(seg(1)).astype(BF16)
    vT_ref[0] = seg(2).T.astype(BF16)
    nq_ref[0] = (seg(3) * HEAD_W ** -0.5).astype(BF16)
    nk_ref[0] = seg(4).astype(BF16)
    nv_ref[0] = seg(5).astype(BF16)
    su_ref[0] = seg(6)
    rq_ref[0] = seg(7).astype(BF16)
    rk_ref[0] = (seg(8) * HEAD_W ** -0.5).astype(BF16)
    rv_ref[0] = seg(9).astype(BF16)
    rg_ref[0] = seg(10)


def _proj_call(geo, x, g, sc, sh, w_in, rope):
    b, l, d = x.shape
    tm = geo.tm
    tok = pl.BlockSpec((1, tm, BRANCH_W), lambda bb, i: (bb, i, 0))
    tokT = pl.BlockSpec((1, BRANCH_W, tm), lambda bb, i: (bb, 0, i))
    vec = pl.BlockSpec((1, 1, d), lambda bb, i: (bb, 0, 0))
    tab = pl.BlockSpec((tm, 128), lambda bb, i: (i, 0))
    su_spec = pl.BlockSpec((1, tm, BRANCH_W), lambda bb, i: geo.scan_index(bb, i))
    bl = jax.ShapeDtypeStruct((b, l, BRANCH_W), BF16)
    blT = jax.ShapeDtypeStruct((b, BRANCH_W, l), BF16)
    blf = jax.ShapeDtypeStruct((b, l, BRANCH_W), F32)
    su_shape = jax.ShapeDtypeStruct((2, geo.lseg, 8 * BRANCH_W), F32)
    return pl.pallas_call(
        _proj_kernel,
        grid=(b, geo.nt),
        in_specs=[pl.BlockSpec((1, tm, d), lambda bb, i: (bb, i, 0)),
                  _const_spec((1, d)), vec, vec,
                  _const_spec((d, N_SEG * BRANCH_W)), tab, tab, tab],
        out_specs=[tokT, tok, tokT, tok, tok, tok, su_spec, tok, tok, tok, tok],
        out_shape=[blT, bl, blT, bl, bl, bl, su_shape, bl, bl, bl, blf],
        compiler_params=_cparams(("parallel", "parallel")),
        name="proj",
    )(x, g, sc, sh, w_in, *rope)


def _rope_tables(l):
    half = DA_DIM // 4 // 2
    inv_freq = jnp.power(jnp.float32(ROPE_THETA), -jnp.arange(half, dtype=F32) / half)
    ang = jnp.arange(l).astype(F32)[:, None] * inv_freq[None, :]
    cos, sin = jnp.cos(ang), jnp.sin(ang)
    j = np.arange(128) % DA_DIM
    idx = np.where(j < half, j, np.where(j < 2 * half, j - half, 0))
    in_a = jnp.asarray(j < half)[None, :]
    in_b = jnp.asarray((j >= half) & (j < 2 * half))[None, :]
    cos_t = jnp.where(in_a | in_b, cos[:, idx], 1.0)
    sina_t = jnp.where(in_a, -sin[:, idx], 0.0)
    sinb_t = jnp.where(in_b, sin[:, idx], 0.0)
    return cos_t, sina_t, sinb_t


def _da_kernel(lam_ref, g_ref, qT_ref, k_ref, vT_ref, o_ref,
               qm_scr, m_scr, l_scr, acc_scr, *, lam_init, nkv):
    kv = pl.program_id(2)

    @pl.when(kv == 0)
    def _():
        qT = qT_ref[0]
        row = lax.broadcasted_iota(jnp.int32, qT.shape, 0)
        for hc in range(2 * N_HEADS):
            keep = (row >= hc * DA_DIM) & (row < (hc + 1) * DA_DIM)
            qm_scr[hc] = jnp.where(keep, qT, jnp.zeros_like(qT))
        m_scr[...] = jnp.full(m_scr.shape, NEG, F32)
        l_scr[...] = jnp.zeros(l_scr.shape, F32)
        acc_scr[...] = jnp.zeros(acc_scr.shape, F32)

    kb = k_ref[0]
    for hc in range(2 * N_HEADS):
        h = hc // 2
        s_t = jnp.dot(kb, qm_scr[hc], preferred_element_type=F32)
        m_old = m_scr[hc]
        m_new = jnp.maximum(m_old, jnp.max(s_t, axis=0, keepdims=True))
        p = jnp.exp2(s_t - m_new)
        alpha = jnp.exp2(m_old - m_new)
        l_scr[hc] = alpha * l_scr[hc] + jnp.sum(p, axis=0, keepdims=True)
        pv = jnp.dot(vT_ref[0, h * HEAD_W:(h + 1) * HEAD_W, :], p.astype(BF16),
                     preferred_element_type=F32)
        acc_scr[hc] = alpha * acc_scr[hc] + pv
        m_scr[hc] = m_new

    @pl.when(kv == nkv - 1)
    def _():
        lp = lam_ref[...]
        lam = (jnp.exp(jnp.sum(lp[0:1] * lp[1:2], axis=-1, keepdims=True))
               - jnp.exp(jnp.sum(lp[2:3] * lp[3:4], axis=-1, keepdims=True)) + lam_init)
        for h in range(N_HEADS):
            o1 = acc_scr[2 * h] / l_scr[2 * h]
            o2 = acc_scr[2 * h + 1] / l_scr[2 * h + 1]
            o = o1 - lam * o2
            ms = jnp.mean(o * o, axis=0, keepdims=True)
            y = o * lax.rsqrt(ms + EPS) * g_ref[h * HEAD_W:(h + 1) * HEAD_W, :]
            o_ref[0, h * HEAD_W:(h + 1) * HEAD_W, :] = y * (1.0 - lam_init)


def _da_call(qT, k, vT, da_lambda, subln_g, lam_init):
    b, _, l = qT.shape
    tq = min(512, l)
    tk = min(1024, l)
    nkv = l // tk
    g_col = jnp.tile(subln_g, N_HEADS).reshape(BRANCH_W, 1)
    return pl.pallas_call(
        functools.partial(_da_kernel, lam_init=lam_init, nkv=nkv),
        grid=(b, l // tq, nkv),
        in_specs=[_const_spec((4, DA_DIM)), _const_spec((BRANCH_W, 1)),
                  pl.BlockSpec((1, BRANCH_W, tq), lambda bb, i, j: (bb, 0, i)),
                  pl.BlockSpec((1, tk, BRANCH_W), lambda bb, i, j: (bb, j, 0)),
                  pl.BlockSpec((1, BRANCH_W, tk), lambda bb, i, j: (bb, 0, j))],
        out_specs=pl.BlockSpec((1, BRANCH_W, tq), lambda bb, i, j: (bb, 0, i)),
        out_shape=jax.ShapeDtypeStruct((b, BRANCH_W, l), F32),
        scratch_shapes=[pltpu.VMEM((2 * N_HEADS, BRANCH_W, tq), BF16),
                        pltpu.VMEM((2 * N_HEADS, 1, tq), F32),
                        pltpu.VMEM((2 * N_HEADS, 1, tq), F32),
                        pltpu.VMEM((2 * N_HEADS, HEAD_W, tq), F32)],
        compiler_params=_cparams(("parallel", "parallel", "arbitrary")),
        name="diff_attn",
    )(da_lambda, g_col, qT, k, vT)


NA_RB = 8
NA_WIN = NA_ROWS * GRID_W


def _na_kernel(q_ref, k_ref, v_ref, bias_ref, o_ref, *, rows):
    i = pl.program_id(1)
    lane_q = lax.broadcasted_iota(jnp.int32, (GRID_W, BRANCH_W), 1) // HEAD_W
    for j in range(NA_RB):
        r = i * NA_RB + j
        rs = jnp.clip(r - NA_ROWS // 2, 0, rows - NA_ROWS)
        start = pl.multiple_of(rs * GRID_W, GRID_W)
        kw = k_ref[0, pl.ds(start, NA_WIN), :]
        vw = v_ref[0, pl.ds(start, NA_WIN), :]
        q = q_ref[0, j * GRID_W:(j + 1) * GRID_W, :]
        qm = jnp.concatenate(
            [jnp.where(lane_q == h, q, jnp.zeros_like(q)) for h in range(N_HEADS)], axis=0)
        s = lax.dot_general(qm, kw, (((1,), (1,)), ((), ())), preferred_element_type=F32)
        s = s + bias_ref[r - rs]
        m = jnp.max(s, axis=-1, keepdims=True)
        e = jnp.exp(s - m)
        den = jnp.sum(e, axis=-1, keepdims=True)
        o_all = jnp.dot(e.astype(BF16), vw, preferred_element_type=F32) / den
        o = jnp.zeros((GRID_W, BRANCH_W), F32)
        for h in range(N_HEADS):
            o = o + jnp.where(lane_q == h, o_all[h * GRID_W:(h + 1) * GRID_W, :], 0.0)
        o_ref[0, j * GRID_W:(j + 1) * GRID_W, :] = o


def _na_call(geo, nq, nk, nv, bias):
    b, l, _ = nq.shape
    blk = NA_RB * GRID_W
    full = pl.BlockSpec((1, l, BRANCH_W), lambda bb, i: (bb, 0, 0))
    return pl.pallas_call(
        functools.partial(_na_kernel, rows=geo.rows),
        grid=(b, l // blk),
        in_specs=[pl.BlockSpec((1, blk, BRANCH_W), lambda bb, i: (bb, i, 0)), full, full,
                  _const_spec(bias.shape)],
        out_specs=pl.BlockSpec((1, blk, BRANCH_W), lambda bb, i: (bb, i, 0)),
        out_shape=jax.ShapeDtypeStruct((b, l, BRANCH_W), F32),
        compiler_params=_cparams(("parallel", "arbitrary")),
        name="nbr_attn",
    )(nq, nk, nv, bias)


def _na_bias_table(rpb):
    w = np.arange(GRID_W)
    cs = np.clip(w - NA_COLS // 2, 0, GRID_W - NA_COLS)
    kc = np.arange(GRID_W)
    allowed = (kc[None, :] >= cs[:, None]) & (kc[None, :] < cs[:, None] + NA_COLS)
    dcol = np.clip(kc[None, :] - w[:, None] + (NA_COLS - 1), 0, 2 * NA_COLS - 2)
    delta = np.arange(NA_ROWS)
    wi = np.arange(NA_ROWS)
    drow = wi[None, :] - delta[:, None] + (NA_ROWS - 1)
    t = rpb.astype(F32)[:, drow[:, :, None, None], dcol[None, None, :, :]]
    t = jnp.where(jnp.asarray(allowed)[None, None, None], t, NEG)
    t = t.transpose(1, 0, 3, 2, 4)
    return t.reshape(NA_ROWS, N_HEADS * GRID_W, NA_WIN)


S5_W2 = 2 * S5_G * S5_P


def _s5_kernel(*refs, tc, nch, emit_y):
    if emit_y:
        su_ref, b_ref, a_ref, c_ref, init_ref, y_ref, fin_ref, bu_scr, st_scr = refs
    else:
        su_ref, b_ref, a_ref, init_ref, fin_ref, bu_scr, st_scr = refs
    d = pl.program_id(0)
    c = pl.program_id(2)
    half = S5_W2 // 2

    @pl.when(c == 0)
    def _():
        st_scr[...] = init_ref[0, 0]

    bu_scr[...] = jnp.dot(su_ref[0].astype(BF16), b_ref[0], preferred_element_type=F32)
    a = a_ref[0]
    ar = jnp.broadcast_to(a[:, :half], (8, half))
    ai = jnp.broadcast_to(a[:, half:], (8, half))

    def body(t, carry):
        sr, si = carry
        tt = jnp.where(d == 0, t, tc - 1 - t)
        off = pl.multiple_of(tt * 8, 8)
        br = bu_scr[pl.ds(off, 8), 0:half]
        bi = bu_scr[pl.ds(off, 8), half:S5_W2]
        nr = ar * sr - ai * si + br
        ni = ar * si + ai * sr + bi
        if emit_y:
            bu_scr[pl.ds(off, 8), 0:half] = nr
            bu_scr[pl.ds(off, 8), half:S5_W2] = ni
        return nr, ni

    sr, si = lax.fori_loop(0, tc, body, (st_scr[:, 0:half], st_scr[:, half:S5_W2]))
    st_scr[:, 0:half] = sr
    st_scr[:, half:S5_W2] = si
    if emit_y:
        y_ref[0, 0] = jnp.dot(bu_scr[...].astype(BF16), c_ref[0], preferred_element_type=F32)

    @pl.when(c == nch - 1)
    def _():
        fin_ref[0, 0] = st_scr[...]


def _s5_call(geo, su, bcat, a, ccat, init, emit_y):
    lseg = geo.lseg
    tc = min(128, lseg)
    nch = lseg // tc
    su2 = su.reshape(2, lseg * 8, BRANCH_W)

    def chunk(d, c):
        return jnp.where(d == 0, c, nch - 1 - c)

    in_specs = [pl.BlockSpec((1, tc * 8, BRANCH_W), lambda d, hf, c: (hf, chunk(d, c), 0)),
                pl.BlockSpec((1, BRANCH_W, S5_W2), lambda d, hf, c: (d, 0, 0)),
                pl.BlockSpec((1, 1, S5_W2), lambda d, hf, c: (d, 0, 0))]
    args = [su2, bcat, a]
    if emit_y:
        in_specs.append(pl.BlockSpec((1, S5_W2, BRANCH_W), lambda d, hf, c: (d, 0, 0)))
        args.append(ccat)
    st_spec = pl.BlockSpec((1, 1, 8, S5_W2), lambda d, hf, c: (d, hf, 0, 0))
    in_specs.append(st_spec)
    args.append(init)
    st_shape = jax.ShapeDtypeStruct((2, 2, 8, S5_W2), F32)
    if emit_y:
        out_specs = [pl.BlockSpec((1, 1, tc * 8, BRANCH_W), lambda d, hf, c: (d, hf, chunk(d, c), 0)),
                     st_spec]
        out_shape = [jax.ShapeDtypeStruct((2, 2, lseg * 8, BRANCH_W), F32), st_shape]
    else:
        out_specs = [st_spec]
        out_shape = [st_shape]
    outs = pl.pallas_call(
        functools.partial(_s5_kernel, tc=tc, nch=nch, emit_y=emit_y),
        grid=(2, 2, nch),
        in_specs=in_specs, out_specs=out_specs, out_shape=out_shape,
        scratch_shapes=[pltpu.VMEM((tc * 8, S5_W2), F32), pltpu.VMEM((8, S5_W2), F32)],
        compiler_params=_cparams(("arbitrary", "arbitrary", "arbitrary")),
        name="s5_scan" if emit_y else "s5_endstate",
    )(*args)
    return outs


def _s5_carry_kernel(fin_ref, a_ref, init_ref, *, lseg):
    half = S5_W2 // 2
    for d in range(2):
        a = a_ref[d]
        pr, pi = a[:, :half], a[:, half:]
        for _ in range(int(math.log2(lseg))):
            pr, pi = pr * pr - pi * pi, 2.0 * pr * pi
        for hf in range(2):
            sr = jnp.zeros((1, half), F32)
            si = jnp.zeros((1, half), F32)
            order = range(8) if d == 0 else range(7, -1, -1)
            for s in order:
                init_ref[d, hf, s:s + 1, 0:half] = sr
                init_ref[d, hf, s:s + 1, half:S5_W2] = si
                fr = fin_ref[d, hf, s:s + 1, 0:half]
                fi = fin_ref[d, hf, s:s + 1, half:S5_W2]
                sr, si = pr * sr - pi * si + fr, pr * si + pi * sr + fi


def _s5_carry_call(geo, fin, a):
    return pl.pallas_call(
        functools.partial(_s5_carry_kernel, lseg=geo.lseg),
        out_shape=jax.ShapeDtypeStruct(fin.shape, F32),
        name="s5_carry",
    )(fin, a)


def _s5_mixer(geo, su, s5w):
    bcat, a, ccat = s5w
    zeros = jnp.zeros((2, 2, 8, S5_W2), F32)
    if geo.spb == 1:
        init = zeros
    else:
        assert geo.spb == 8
        (fin,) = _s5_call(geo, su, bcat, a, None, zeros, emit_y=False)
        init = _s5_carry_call(geo, fin, a)
    y, _ = _s5_call(geo, su, bcat, a, ccat, init, emit_y=True)
    return y.reshape(2, 2, geo.lseg, 8 * BRANCH_W)


def _s5_weights(a_re, a_im, log_dt, b_re, b_im, c_re, c_im):
    lam_re = jnp.minimum(a_re.astype(F32), -1e-4)
    lam_im = a_im.astype(F32)
    dt = jnp.exp(log_dt.astype(F32))[..., None]
    mag = jnp.exp(lam_re * dt)
    abar_re = mag * jnp.cos(lam_im * dt)
    abar_im = mag * jnp.sin(lam_im * dt)
    den = lam_re * lam_re + lam_im * lam_im
    f_re = ((abar_re - 1.0) * lam_re + abar_im * lam_im) / den
    f_im = (abar_im * lam_re - (abar_re - 1.0) * lam_im) / den
    br, bi = b_re.astype(F32), b_im.astype(F32)
    bb_re = f_re[..., None] * br - f_im[..., None] * bi
    bb_im = f_re[..., None] * bi + f_im[..., None] * br
    eye = jnp.eye(S5_G, dtype=F32)

    def bd_in(m):
        return jnp.einsum('dgph,gk->dghkp', m, eye).reshape(2, S5_G * S5_H, S5_G * S5_P)

    def bd_out(m):
        return jnp.einsum('dghp,gk->dgpkh', m, eye).reshape(2, S5_G * S5_P, S5_G * S5_H)

    bcat = jnp.concatenate([bd_in(bb_re), bd_in(bb_im)], axis=-1).astype(BF16)
    ccat = jnp.concatenate([bd_out(c_re.astype(F32)), -bd_out(c_im.astype(F32))], axis=1).astype(BF16)
    a = jnp.concatenate([abar_re.reshape(2, 1, -1), abar_im.reshape(2, 1, -1)], axis=-1)
    return bcat, a, ccat


def _ret_kernel(qf_ref, kf_ref, vf_ref, qb_ref, kb_ref, vb_ref, dec_ref, xi_ref, zeta_ref, cd_ref,
                of_ref, ob_ref, st_scr):
    i = pl.program_id(1)
    cr = qf_ref.shape[1]

    @pl.when(i == 0)
    def _():
        st_scr[...] = jnp.zeros(st_scr.shape, F32)

    lane = lax.broadcasted_iota(jnp.int32, (cr, BRANCH_W), 1) // HEAD_W
    blk = (lax.broadcasted_iota(jnp.int32, (BRANCH_W, BRANCH_W), 0) // HEAD_W
           == lax.broadcasted_iota(jnp.int32, (BRANCH_W, BRANCH_W), 1) // HEAD_W)
    dirs = ((qf_ref, kf_ref, vf_ref, of_ref), (qb_ref, kb_ref, vb_ref, ob_ref))
    for d, (q_ref, k_ref, v_ref, o_ref) in enumerate(dirs):
        q, k, v = q_ref[0], k_ref[0], v_ref[0]
        state = st_scr[d]
        o = jnp.dot(q, state.astype(BF16), preferred_element_type=F32) * xi_ref[d]
        for h in range(N_HEADS):
            qh = jnp.where(lane == h, q, jnp.zeros_like(q))
            vh = jnp.where(lane == h, v, jnp.zeros_like(v))
            s = lax.dot_general(qh, k, (((1,), (1,)), ((), ())), preferred_element_type=F32)
            s = s * dec_ref[d, h]
            o = o + jnp.dot(s.astype(BF16), vh, preferred_element_type=F32)
        o_ref[0] = o
        kz = (k.astype(F32) * zeta_ref[d]).T.astype(BF16)
        kv = jnp.dot(kz, v, preferred_element_type=F32)
        st_scr[d] = state * cd_ref[d] + jnp.where(blk, kv, 0.0)


def _ret_call(rq, rk, rv, tabs):
    b, l, _ = rq.shape
    cr = tabs[0].shape[-1]
    n = l // cr
    fwd = pl.BlockSpec((1, cr, BRANCH_W), lambda bb, i: (bb, i, 0))
    bwd = pl.BlockSpec((1, cr, BRANCH_W), lambda bb, i: (bb, n - 1 - i, 0))
    shp = jax.ShapeDtypeStruct((b, l, BRANCH_W), F32)
    return pl.pallas_call(
        _ret_kernel,
        grid=(b, n),
        in_specs=[fwd, fwd, fwd, bwd, bwd, bwd] + [_const_spec(t.shape) for t in tabs],
        out_specs=[fwd, bwd],
        out_shape=[shp, shp],
        scratch_shapes=[pltpu.VMEM((2, BRANCH_W, BRANCH_W), F32)],
        compiler_params=_cparams(("parallel", "arbitrary")),
        name="retention",
    )(rq, rk, rv, rq, rk, rv, *tabs)


def _ret_tables(ret_log_decay, cr):
    lg = -jnp.exp(ret_log_decay.astype(F32))
    idx = jnp.arange(cr, dtype=F32)
    rel = idx[:, None] - idx[None, :]
    m_f = rel >= 0
    m_b = rel < 0
    dec_f = jnp.where(m_f[None], jnp.exp(lg[0][:, None, None] * jnp.where(m_f, rel, 0.0)[None]), 0.0)
    dec_b = jnp.where(m_b[None], jnp.exp(lg[1][:, None, None] * jnp.where(m_b, -rel, 0.0)[None]), 0.0)
    dec = jnp.stack([dec_f, dec_b])
    lane_lg = jnp.repeat(lg, HEAD_W, axis=1)
    xi = jnp.stack([jnp.exp(lane_lg[0][None, :] * (idx + 1.0)[:, None]),
                    jnp.exp(lane_lg[1][None, :] * (cr - idx)[:, None])])
    zeta = jnp.stack([jnp.exp(lane_lg[0][None, :] * (cr - 1.0 - idx)[:, None]),
                      jnp.exp(lane_lg[1][None, :] * idx[:, None])])
    cd = jnp.exp(lane_lg * cr)[:, None, :]
    return dec, xi, zeta, cd


def _merge_kernel(x_ref, g_ref, sc_ref, sh_ref, g1_ref,
                  oaT_ref, ob_ref, su_ref, yf_ref, yb_ref, rf_ref, rb_ref, rg_ref,
                  s5d_ref, gluw_ref, glub_ref, rng_ref, ones_ref,
                  wg_ref, wb_ref, wo_ref, o_ref):
    x = x_ref[0]
    h = _norm_mod(x, g_ref[...], sc_ref[0], sh_ref[0]).astype(BF16)

    o_a = oaT_ref[0].T
    o_b = ob_ref[0]

    y = s5d_ref[...] * su_ref[0] + yf_ref[0, 0] + yb_ref[0, 0]
    y = jax.nn.gelu(y)
    o_c = y * jax.nn.sigmoid(jnp.dot(y.astype(BF16), gluw_ref[...], preferred_element_type=F32)
                             + glub_ref[...])

    z = rf_ref[0] + rb_ref[0]
    z2 = z * z
    hi = z2.astype(BF16)
    lo = (z2 - hi.astype(F32)).astype(BF16)
    ss = (jnp.dot(hi, ones_ref[...], preferred_element_type=F32)
          + jnp.dot(lo, ones_ref[...], preferred_element_type=F32))
    rg = rg_ref[0]
    o_d = z * lax.rsqrt(ss * (1.0 / HEAD_W) + EPS) * rng_ref[...] * (rg * jax.nn.sigmoid(rg))

    merged = None
    for i, br in enumerate((o_a, o_b, o_c, o_d)):
        gate = jax.nn.sigmoid(jnp.dot(h, wg_ref[i], preferred_element_type=F32))
        t = gate * jnp.dot(br.astype(BF16), wb_ref[i], preferred_element_type=F32)
        merged = t if merged is None else merged + t
    out = jnp.dot(merged.astype(BF16), wo_ref[...], preferred_element_type=F32)
    o_ref[0] = x + g1_ref[0] * out


def _merge_call(geo, x, g, sc, sh, g1, oaT, ob, su, y, rf, rb, rg, mw):
    b, l, d = x.shape
    tm = geo.tm
    tok = pl.BlockSpec((1, tm, BRANCH_W), lambda bb, i: (bb, i, 0))
    xs = pl.BlockSpec((1, tm, d), lambda bb, i: (bb, i, 0))
    vec = pl.BlockSpec((1, 1, d), lambda bb, i: (bb, 0, 0))
    su_spec = pl.BlockSpec((1, tm, BRANCH_W), lambda bb, i: geo.scan_index(bb, i))

    def y_spec(dd):
        return pl.BlockSpec((1, 1, tm, BRANCH_W), lambda bb, i: (dd,) + geo.scan_index(bb, i))

    s5_d, glu_w, glu_b, ret_g, ones_bd, wg, wb, wo = mw
    return pl.pallas_call(
        _merge_kernel,
        grid=(b, geo.nt),
        in_specs=[xs, _const_spec((1, d)), vec, vec, vec,
                  pl.BlockSpec((1, BRANCH_W, tm), lambda bb, i: (bb, 0, i)), tok,
                  su_spec, y_spec(0), y_spec(1), tok, tok, tok,
                  _const_spec(s5_d.shape), _const_spec(glu_w.shape), _const_spec(glu_b.shape),
                  _const_spec(ret_g.shape), _const_spec(ones_bd.shape),
                  _const_spec(wg.shape), _const_spec(wb.shape), _const_spec(wo.shape)],
        out_specs=xs,
        out_shape=jax.ShapeDtypeStruct(x.shape, F32),
        compiler_params=_cparams(("parallel", "parallel")),
        name="merge",
    )(x, g, sc, sh, g1, oaT, ob, su, y, y, rf, rb, rg, s5_d, glu_w, glu_b, ret_g, ones_bd, wg, wb, wo)


HALO = 8


def _ffn_kernel(xp_ref, x_ref, xn_ref, g_ref, sc_ref, sh_ref, g2_ref,
                wu_ref, wgt_ref, cw_ref, cb_ref, wd_ref, fg_ref, o_ref, a_scr, *, nt, final):
    i = pl.program_id(1)
    tm = x_ref.shape[1]
    g, sc, sh = g_ref[...], sc_ref[0], sh_ref[0]
    x = x_ref[0]
    h = _norm_mod(x, g, sc, sh)
    hp = _norm_mod(xp_ref[0], g, sc, sh) * (i > 0).astype(F32)
    hn = _norm_mod(xn_ref[0], g, sc, sh) * (i < nt - 1).astype(F32)
    h_ext = jnp.concatenate([hp, h, hn], axis=0).astype(BF16)
    a_scr[...] = jnp.dot(h_ext, wu_ref[...], preferred_element_type=F32)
    cw = cw_ref[...]
    a = (a_scr[pl.ds(HALO - 1, tm), :] * cw[0:1] + a_scr[pl.ds(HALO, tm), :] * cw[1:2]
         + a_scr[pl.ds(HALO + 1, tm), :] * cw[2:3] + cb_ref[...])
    gate = jnp.dot(h.astype(BF16), wgt_ref[...], preferred_element_type=F32)
    act = (jax.nn.gelu(a) * gate).astype(BF16)
    y = x + g2_ref[0] * jnp.dot(act, wd_ref[...], preferred_element_type=F32)
    if final:
        ms = jnp.mean(y * y, axis=-1, keepdims=True)
        y = y * lax.rsqrt(ms + EPS) * fg_ref[...]
    o_ref[0] = y


def _ffn_call(geo, x, g, sc, sh, g2, fw, final_g, final):
    b, l, d = x.shape
    tm = min(geo.tm, 256)
    nt = l // tm
    r = tm // HALO
    nh = l // HALO
    xs = pl.BlockSpec((1, tm, d), lambda bb, i: (bb, i, 0))
    prev = pl.BlockSpec((1, HALO, d), lambda bb, i: (bb, jnp.maximum(i * r - 1, 0), 0))
    nxt = pl.BlockSpec((1, HALO, d), lambda bb, i: (bb, jnp.minimum((i + 1) * r, nh - 1), 0))
    vec = pl.BlockSpec((1, 1, d), lambda bb, i: (bb, 0, 0))
    wu, wgt, cw, cb, wd = fw
    once = pl.Buffered(1)

    def wspec(shape):
        return pl.BlockSpec(shape, lambda *_: (0,) * len(shape), pipeline_mode=once)

    return pl.pallas_call(
        functools.partial(_ffn_kernel, nt=nt, final=final),
        grid=(b, nt),
        in_specs=[prev, xs, nxt, _const_spec((1, d)), vec, vec, vec,
                  wspec(wu.shape), wspec(wgt.shape), _const_spec(cw.shape), _const_spec(cb.shape),
                  wspec(wd.shape), _const_spec((1, d))],
        out_specs=xs,
        out_shape=jax.ShapeDtypeStruct(x.shape, F32),
        scratch_shapes=[pltpu.VMEM((tm + 2 * HALO, D_FF), F32)],
        compiler_params=_cparams(("parallel", "parallel")),
        name="conv_ffn",
    )(x, x, x, g, sc, sh, g2, wu, wgt, cw, cb, wd, final_g)


RET_CHUNK = 256


def _layer_weights(l, p):
    d = p['w_in'].shape[1]
    ones_bd = jnp.asarray(np.kron(np.eye(N_HEADS), np.ones((HEAD_W, HEAD_W))), BF16)
    return dict(
        norm1_g=p['norm1_g'][l].reshape(1, d),
        norm2_g=p['norm2_g'][l].reshape(1, d),
        w_in=p['w_in'][l].astype(BF16),
        da_lambda=p['da_lambda'][l].astype(F32),
        da_subln_g=p['da_subln_g'][l].astype(F32),
        na_bias=_na_bias_table(p['na_rpb'][l]),
        s5=_s5_weights(p['s5_a_re'][l], p['s5_a_im'][l], p['s5_log_dt'][l], p['s5_b_re'][l],
                       p['s5_b_im'][l], p['s5_c_re'][l], p['s5_c_im'][l]),
        ret=_ret_tables(p['ret_log_decay'][l], RET_CHUNK),
        merge=(p['s5_d'][l].reshape(1, BRANCH_W).astype(F32),
               p['s5_glu_w'][l].astype(BF16),
               p['s5_glu_b'][l].reshape(1, BRANCH_W).astype(F32),
               jnp.tile(p['ret_norm_g'][l].astype(F32), N_HEADS).reshape(1, BRANCH_W),
               ones_bd,
               p['w_branch_gate'][l].astype(BF16),
               p['w_branch'][l].astype(BF16),
               p['w_out'][l].astype(BF16)),
        ffn=(p['ffn_w_up'][l].astype(BF16),
             p['ffn_w_gate'][l].astype(BF16),
             p['ffn_conv_w'][l].reshape(3, D_FF).astype(F32),
             p['ffn_conv_b'][l].reshape(1, D_FF).astype(F32),
             p['ffn_w_down'][l].astype(BF16)),
    )


def _ret_tabs_for(geo, ret_log_decay_l):
    return _ret_tables(ret_log_decay_l, min(RET_CHUNK, geo.l))


def _layer(geo, x, mod, lw, lam_init, rope, final_g, final):
    b, l, d = x.shape
    sh1, sc1, g1, sh2, sc2, g2 = [mod[:, k * d:(k + 1) * d].reshape(b, 1, d) for k in range(6)]
    (qT, k, vT, nq, nk, nv, su, rq, rk, rv, rg) = _proj_call(
        geo, x, lw['norm1_g'], sc1, sh1, lw['w_in'], rope)
    oaT = _da_call(qT, k, vT, lw['da_lambda'], lw['da_subln_g'], lam_init)
    ob = _na_call(geo, nq, nk, nv, lw['na_bias'])
    y = _s5_mixer(geo, su, lw['s5'])
    rf, rb = _ret_call(rq, rk, rv, lw['ret'])
    x = _merge_call(geo, x, lw['norm1_g'], sc1, sh1, g1, oaT, ob, su, y, rf, rb, rg, lw['merge'])
    return _ffn_call(geo, x, lw['norm2_g'], sc2, sh2, g2, lw['ffn'], final_g, final)


def kernel(x_prompt, x_sample, c_prompt, c_sample, norm1_g, norm2_g, ada_w, ada_b, w_in, da_lambda, da_subln_g, na_rpb, s5_a_re, s5_a_im, s5_log_dt, s5_b_re, s5_b_im, s5_c_re, s5_c_im, s5_d, s5_glu_w, s5_glu_b, ret_log_decay, ret_norm_g, w_branch, w_branch_gate, w_out, ffn_w_up, ffn_w_gate, ffn_conv_w, ffn_conv_b, ffn_w_down, final_norm_g):
    p = dict(norm1_g=norm1_g, norm2_g=norm2_g, w_in=w_in, da_lambda=da_lambda,
             da_subln_g=da_subln_g, na_rpb=na_rpb, s5_a_re=s5_a_re, s5_a_im=s5_a_im,
             s5_log_dt=s5_log_dt, s5_b_re=s5_b_re, s5_b_im=s5_b_im, s5_c_re=s5_c_re,
             s5_c_im=s5_c_im, s5_d=s5_d, s5_glu_w=s5_glu_w, s5_glu_b=s5_glu_b,
             ret_log_decay=ret_log_decay, ret_norm_g=ret_norm_g, w_branch=w_branch,
             w_branch_gate=w_branch_gate, w_out=w_out, ffn_w_up=ffn_w_up, ffn_w_gate=ffn_w_gate,
             ffn_conv_w=ffn_conv_w, ffn_conv_b=ffn_conv_b, ffn_w_down=ffn_w_down)
    depth = w_in.shape[0]
    d = w_in.shape[1]
    nbp = c_prompt.shape[0]
    mod = _mod_call(jnp.concatenate([c_prompt, c_sample], axis=0), ada_w, ada_b)
    lws = [_layer_weights(l, p) for l in range(depth)]
    final_g = final_norm_g.reshape(1, d).astype(F32)
    outs = []
    for x, lo in ((x_prompt, 0), (x_sample, nbp)):
        b, l, _ = x.shape
        geo = _Geom(b, l)
        rope = _rope_tables(l)
        for li in range(depth):
            lam_init = 0.8 - 0.6 * math.exp(-0.3 * li)
            x = _layer(geo, x, mod[li, lo:lo + b], lws[li], lam_init, rope, final_g,
                       final=(li == depth - 1))
        outs.append(x)
    return tuple(outs)
```

```python
import functools
import math

import numpy as np
import jax
import jax.numpy as jnp
from jax import lax
from jax.experimental import pallas as pl
from jax.experimental.pallas import tpu as pltpu

F32 = jnp.float32
BF16 = jnp.bfloat16

EPS = 1e-6
ROPE_THETA = 500000.0
BRANCH_W = 256
N_SEG = 11
DA_DIM = 32
HEAD_W = 64
N_HEADS = 4
GRID_W = 64
NA_ROWS = 8
NA_COLS = 16
S5_G = 16
S5_P = 64
S5_H = 16
S5_NSEQ = 16
D_FF = 2816
NEG = -1e30
LOG2E = 1.4426950408889634

VMEM_LIMIT = 56 * 1024 * 1024


def _cparams(sem):
    return pltpu.CompilerParams(dimension_semantics=sem, vmem_limit_bytes=VMEM_LIMIT)


def _const_spec(shape):
    nd = len(shape)
    return pl.BlockSpec(shape, lambda *_: (0,) * nd)


def _norm_mod(x, g, sc, sh):
    ms = jnp.mean(x * x, axis=-1, keepdims=True)
    y = x * lax.rsqrt(ms + EPS) * g
    return y * (1.0 + sc) + sh


def _mod_kernel(c_ref, w_ref, b_ref, o_ref):
    c = c_ref[...]
    s = c * jax.nn.sigmoid(c)
    o_ref[0] = jnp.dot(s.astype(BF16), w_ref[0].astype(BF16),
                       preferred_element_type=F32) + b_ref[0]


def _mod_call(c_all, ada_w, ada_b):
    depth, d, n6 = ada_w.shape
    nb = c_all.shape[0]
    tn = 1536
    return pl.pallas_call(
        _mod_kernel,
        grid=(depth, n6 // tn),
        in_specs=[pl.BlockSpec((nb, d), lambda l, j: (0, 0)),
                  pl.BlockSpec((1, d, tn), lambda l, j: (l, 0, j)),
                  pl.BlockSpec((1, 1, tn), lambda l, j: (l, 0, j))],
        out_specs=pl.BlockSpec((1, nb, tn), lambda l, j: (l, 0, j)),
        out_shape=jax.ShapeDtypeStruct((depth, nb, n6), F32),
        compiler_params=_cparams(("arbitrary", "arbitrary")),
        name="adaln_mod",
    )(c_all, ada_w, ada_b.reshape(depth, 1, n6))


class _Geom:
    def __init__(self, b, l):
        assert S5_NSEQ % b == 0
        self.b, self.l = b, l
        self.spb = S5_NSEQ // b
        assert l % self.spb == 0
        self.lseg = l // self.spb
        assert self.lseg & (self.lseg - 1) == 0
        self.tm = min(512, self.lseg)
        assert self.lseg % self.tm == 0
        self.tps = self.lseg // self.tm
        self.nt = l // self.tm
        self.rows = l // GRID_W
        assert self.rows >= NA_ROWS and l % (NA_ROWS * GRID_W) == 0

    def scan_index(self, b, i):
        n = b * self.spb + i // self.tps
        return n // 8, i % self.tps, n % 8


def _proj_kernel(x_ref, g_ref, sc_ref, sh_ref, w_ref, cos_ref, sina_ref, sinb_ref,
                 qT_ref, k_ref, vT_ref, nq_ref, nk_ref, nv_ref, su_ref,
                 rq_ref, rk_ref, rv_ref, rg_ref):
    h = _norm_mod(x_ref[0], g_ref[...], sc_ref[0], sh_ref[0]).astype(BF16)

    def seg(j):
        return jnp.dot(h, w_ref[:, j * BRANCH_W:(j + 1) * BRANCH_W],
                       preferred_element_type=F32)

    cos, sina, sinb = cos_ref[...], sina_ref[...], sinb_ref[...]

    def rot(p):
        halves = []
        for s in range(2):
            ph = p[:, s * 128:(s + 1) * 128]
            halves.append(ph * cos + pltpu.roll(ph, 124, 1) * sina + pltpu.roll(ph, 4, 1) * sinb)
        return jnp.concatenate(halves, axis=1)

    qscale = DA_DIM ** -0.5 * LOG2E
    qT_ref[0] = (rot(seg(0)) * qscale).T.astype(BF16)
    k_ref[0] = rot(seg(1)).astype(BF16)
    vT_ref[0] = seg(2).T.astype(BF16)
    nq_ref[0] = (seg(3) * HEAD_W ** -0.5).astype(BF16)
    nk_ref[0] = seg(4).astype(BF16)
    nv_ref[0] = seg(5).astype(BF16)
    su_ref[0] = seg(6)
    rq_ref[0] = seg(7).astype(BF16)
    rk_ref[0] = (seg(8) * HEAD_W ** -0.5).astype(BF16)
    rv_ref[0] = seg(9).astype(BF16)
    rg_ref[0] = seg(10)


def _proj_call(geo, x, g, sc, sh, w_in, rope):
    b, l, d = x.shape
    tm = geo.tm
    tok = pl.BlockSpec((1, tm, BRANCH_W), lambda bb, i: (bb, i, 0))
    tokT = pl.BlockSpec((1, BRANCH_W, tm), lambda bb, i: (bb, 0, i))
    vec = pl.BlockSpec((1, 1, d), lambda bb, i: (bb, 0, 0))
    tab = pl.BlockSpec((tm, 128), lambda bb, i: (i, 0))
    su_spec = pl.BlockSpec((1, tm, BRANCH_W), lambda bb, i: geo.scan_index(bb, i))
    bl = jax.ShapeDtypeStruct((b, l, BRANCH_W), BF16)
    blT = jax.ShapeDtypeStruct((b, BRANCH_W, l), BF16)
    blf = jax.ShapeDtypeStruct((b, l, BRANCH_W), F32)
    su_shape = jax.ShapeDtypeStruct((2, geo.lseg, 8 * BRANCH_W), F32)
    return pl.pallas_call(
        _proj_kernel,
        grid=(b, geo.nt),
        in_specs=[pl.BlockSpec((1, tm, d), lambda bb, i: (bb, i, 0)),
                  _const_spec((1, d)), vec, vec,
                  _const_spec((d, N_SEG * BRANCH_W)), tab, tab, tab],
        out_specs=[tokT, tok, tokT, tok, tok, tok, su_spec, tok, tok, tok, tok],
        out_shape=[blT, bl, blT, bl, bl, bl, su_shape, bl, bl, bl, blf],
        compiler_params=_cparams(("parallel", "parallel")),
        name="proj",
    )(x, g, sc, sh, w_in, *rope)


def _rope_tables(l):
    half = DA_DIM // 4 // 2
    inv_freq = jnp.power(jnp.float32(ROPE_THETA), -jnp.arange(half, dtype=F32) / half)
    ang = jnp.arange(l).astype(F32)[:, None] * inv_freq[None, :]
    cos, sin = jnp.cos(ang), jnp.sin(ang)
    zeros = jnp.zeros((l, half), F32)
    rest = DA_DIM - 2 * half

    def lanes(first, second, fill):
        comp = jnp.concatenate([first, second, jnp.full((l, rest), fill, F32)], axis=1)
        return jnp.tile(comp, (1, 128 // DA_DIM))

    return lanes(cos, cos, 1.0), lanes(-sin, zeros, 0.0), lanes(zeros, sin, 0.0)


DA_TQ = 1024
DA_TK = 1024
DA_SEED = 128
DA_SLACK = 60.0


def _da_kernel(lam_ref, g_ref, qT_ref, k_ref, vT_ref, o_ref,
               qm_scr, m_scr, l_scr, acc_scr, cur_ref, *, lam_init, nkv):
    kv = pl.program_id(2)
    n_sets = 2 * N_HEADS

    @pl.when(kv == 0)
    def _():
        qT = qT_ref[0]
        row = lax.broadcasted_iota(jnp.int32, qT.shape, 0)
        k_seed = k_ref[0, 0:DA_SEED, :]
        for hc in range(n_sets):
            keep = (row >= hc * DA_DIM) & (row < (hc + 1) * DA_DIM)
            qm = jnp.where(keep, qT, jnp.zeros_like(qT))
            qm_scr[hc] = qm
            m_scr[hc] = jnp.max(jnp.dot(k_seed, qm, preferred_element_type=F32),
                                axis=0, keepdims=True)
        l_scr[...] = jnp.zeros(l_scr.shape, F32)
        acc_scr[...] = jnp.zeros(acc_scr.shape, F32)
        cur_ref[0] = 0

    cur = cur_ref[0]
    nxt = 1 - cur
    kb = k_ref[0]

    worst = None
    for hc in range(n_sets):
        h = hc // 2
        m_ref = m_scr[hc]
        s_t = jnp.dot(kb, qm_scr[hc], preferred_element_type=F32)
        over = jnp.max(s_t, axis=0, keepdims=True) - m_ref
        worst = over if worst is None else jnp.maximum(worst, over)
        p = jnp.exp2(s_t - m_ref)
        l_scr[nxt, hc] = l_scr[cur, hc] + jnp.sum(p, axis=0, keepdims=True)
        acc_scr[nxt, hc] = acc_scr[cur, hc] + jnp.dot(
            vT_ref[0, h * HEAD_W:(h + 1) * HEAD_W, :], p.astype(BF16), preferred_element_type=F32)
    within = jnp.max(worst) <= DA_SLACK

    @pl.when(within)
    def _():
        cur_ref[0] = nxt

    @pl.when(jnp.logical_not(within))
    def _():
        for hc in range(n_sets):
            h = hc // 2
            s_t = jnp.dot(kb, qm_scr[hc], preferred_element_type=F32)
            m_old = m_scr[hc]
            m_new = jnp.maximum(m_old, jnp.max(s_t, axis=0, keepdims=True))
            p = jnp.exp2(s_t - m_new)
            alpha = jnp.exp2(m_old - m_new)
            l_scr[cur, hc] = alpha * l_scr[cur, hc] + jnp.sum(p, axis=0, keepdims=True)
            pv = jnp.dot(vT_ref[0, h * HEAD_W:(h + 1) * HEAD_W, :], p.astype(BF16),
                         preferred_element_type=F32)
            acc_scr[cur, hc] = alpha * acc_scr[cur, hc] + pv
            m_scr[hc] = m_new

    @pl.when(kv == nkv - 1)
    def _():
        fin = cur_ref[0]
        lp = lam_ref[...]
        lam = (jnp.exp(jnp.sum(lp[0:1] * lp[1:2], axis=-1, keepdims=True))
               - jnp.exp(jnp.sum(lp[2:3] * lp[3:4], axis=-1, keepdims=True)) + lam_init)
        for h in range(N_HEADS):
            o1 = acc_scr[fin, 2 * h] / l_scr[fin, 2 * h]
            o2 = acc_scr[fin, 2 * h + 1] / l_scr[fin, 2 * h + 1]
            o = o1 - lam * o2
            ms = jnp.mean(o * o, axis=0, keepdims=True)
            y = o * lax.rsqrt(ms + EPS) * g_ref[h * HEAD_W:(h + 1) * HEAD_W, :]
            o_ref[0, h * HEAD_W:(h + 1) * HEAD_W, :] = y * (1.0 - lam_init)


def _da_call(qT, k, vT, da_lambda, subln_g, lam_init):
    b, _, l = qT.shape
    tq = min(DA_TQ, l)
    tk = min(DA_TK, l)
    nkv = l // tk
    g_col = jnp.tile(subln_g, N_HEADS).reshape(BRANCH_W, 1)
    return pl.pallas_call(
        functools.partial(_da_kernel, lam_init=lam_init, nkv=nkv),
        grid=(b, l // tq, nkv),
        in_specs=[_const_spec((4, DA_DIM)), _const_spec((BRANCH_W, 1)),
                  pl.BlockSpec((1, BRANCH_W, tq), lambda bb, i, j: (bb, 0, i)),
                  pl.BlockSpec((1, tk, BRANCH_W), lambda bb, i, j: (bb, j, 0)),
                  pl.BlockSpec((1, BRANCH_W, tk), lambda bb, i, j: (bb, 0, j))],
        out_specs=pl.BlockSpec((1, BRANCH_W, tq), lambda bb, i, j: (bb, 0, i)),
        out_shape=jax.ShapeDtypeStruct((b, BRANCH_W, l), F32),
        scratch_shapes=[pltpu.VMEM((2 * N_HEADS, BRANCH_W, tq), BF16),
                        pltpu.VMEM((2 * N_HEADS, 1, tq), F32),
                        pltpu.VMEM((2, 2 * N_HEADS, 1, tq), F32),
                        pltpu.VMEM((2, 2 * N_HEADS, HEAD_W, tq), F32),
                        pltpu.SMEM((1,), jnp.int32)],
        compiler_params=_cparams(("parallel", "parallel", "arbitrary")),
        name="diff_attn",
    )(da_lambda, g_col, qT, k, vT)


NA_RB = 8
NA_WIN = NA_ROWS * GRID_W


def _na_kernel(q_ref, k_ref, v_ref, bias_ref, o_ref, *, rows):
    i = pl.program_id(1)
    lane_q = lax.broadcasted_iota(jnp.int32, (GRID_W, BRANCH_W), 1) // HEAD_W
    for j in range(NA_RB):
        r = i * NA_RB + j
        rs = jnp.clip(r - NA_ROWS // 2, 0, rows - NA_ROWS)
        start = pl.multiple_of(rs * GRID_W, GRID_W)
        kw = k_ref[0, pl.ds(start, NA_WIN), :]
        vw = v_ref[0, pl.ds(start, NA_WIN), :]
        q = q_ref[0, j * GRID_W:(j + 1) * GRID_W, :]
        qm = jnp.concatenate(
            [jnp.where(lane_q == h, q, jnp.zeros_like(q)) for h in range(N_HEADS)], axis=0)
        s = lax.dot_general(qm, kw, (((1,), (1,)), ((), ())), preferred_element_type=F32)
        s = s + bias_ref[r - rs]
        m = jnp.max(s, axis=-1, keepdims=True)
        e = jnp.exp(s - m)
        den = jnp.sum(e, axis=-1, keepdims=True)
        o_all = jnp.dot(e.astype(BF16), vw, preferred_element_type=F32) / den
        o = jnp.zeros((GRID_W, BRANCH_W), F32)
        for h in range(N_HEADS):
            o = o + jnp.where(lane_q == h, o_all[h * GRID_W:(h + 1) * GRID_W, :], 0.0)
        o_ref[0, j * GRID_W:(j + 1) * GRID_W, :] = o


def _na_call(geo, nq, nk, nv, bias):
    b, l, _ = nq.shape
    blk = NA_RB * GRID_W
    full = pl.BlockSpec((1, l, BRANCH_W), lambda bb, i: (bb, 0, 0))
    return pl.pallas_call(
        functools.partial(_na_kernel, rows=geo.rows),
        grid=(b, l // blk),
        in_specs=[pl.BlockSpec((1, blk, BRANCH_W), lambda bb, i: (bb, i, 0)), full, full,
                  _const_spec(bias.shape)],
        out_specs=pl.BlockSpec((1, blk, BRANCH_W), lambda bb, i: (bb, i, 0)),
        out_shape=jax.ShapeDtypeStruct((b, l, BRANCH_W), F32),
        compiler_params=_cparams(("parallel", "arbitrary")),
        name="nbr_attn",
    )(nq, nk, nv, bias)


def _na_bias_table(rpb):
    w = np.arange(GRID_W)
    cs = np.clip(w - NA_COLS // 2, 0, GRID_W - NA_COLS)
    kc = np.arange(GRID_W)
    allowed = (kc[None, :] >= cs[:, None]) & (kc[None, :] < cs[:, None] + NA_COLS)
    dcol = np.clip(kc[None, :] - w[:, None] + (NA_COLS - 1), 0, 2 * NA_COLS - 2)
    delta = np.arange(NA_ROWS)
    wi = np.arange(NA_ROWS)
    drow = wi[None, :] - delta[:, None] + (NA_ROWS - 1)
    sel_r = jnp.asarray(drow[:, :, None] == np.arange(2 * NA_ROWS - 1), F32)
    sel_c = jnp.asarray(dcol[:, :, None] == np.arange(2 * NA_COLS - 1), F32)
    t = jnp.einsum('xia,hab,wkb->xhwik', sel_r, rpb.astype(F32), sel_c,
                   precision=lax.Precision.HIGHEST)
    t = jnp.where(jnp.asarray(allowed)[None, None, :, None, :], t, NEG)
    return t.reshape(NA_ROWS, N_HEADS * GRID_W, NA_WIN)


S5_W2 = 2 * S5_G * S5_P


def _s5_kernel(*refs, tc, nch, emit_y):
    if emit_y:
        su_ref, b_ref, a_ref, c_ref, init_ref, y_ref, fin_ref, bu_scr, st_scr = refs
    else:
        su_ref, b_ref, a_ref, init_ref, fin_ref, bu_scr, st_scr = refs
    d = pl.program_id(0)
    c = pl.program_id(2)
    half = S5_W2 // 2

    @pl.when(c == 0)
    def _():
        st_scr[...] = init_ref[0, 0]

    bu_scr[...] = jnp.dot(su_ref[0].astype(BF16), b_ref[0], preferred_element_type=F32)
    a = a_ref[0]
    ar = jnp.broadcast_to(a[:, :half], (8, half))
    ai = jnp.broadcast_to(a[:, half:], (8, half))

    def body(t, carry):
        sr, si = carry
        tt = jnp.where(d == 0, t, tc - 1 - t)
        off = pl.multiple_of(tt * 8, 8)
        br = bu_scr[pl.ds(off, 8), 0:half]
        bi = bu_scr[pl.ds(off, 8), half:S5_W2]
        nr = ar * sr - ai * si + br
        ni = ar * si + ai * sr + bi
        if emit_y:
            bu_scr[pl.ds(off, 8), 0:half] = nr
            bu_scr[pl.ds(off, 8), half:S5_W2] = ni
        return nr, ni

    sr, si = lax.fori_loop(0, tc, body, (st_scr[:, 0:half], st_scr[:, half:S5_W2]), unroll=4)
    st_scr[:, 0:half] = sr
    st_scr[:, half:S5_W2] = si
    if emit_y:
        y_ref[0, 0] = jnp.dot(bu_scr[...].astype(BF16), c_ref[0], preferred_element_type=F32)

    @pl.when(c == nch - 1)
    def _():
        fin_ref[0, 0] = st_scr[...]


def _s5_call(geo, su, bcat, a, ccat, init, emit_y):
    lseg = geo.lseg
    tc = min(128, lseg)
    nch = lseg // tc
    su2 = su.reshape(2, lseg * 8, BRANCH_W)

    def chunk(d, c):
        return jnp.where(d == 0, c, nch - 1 - c)

    in_specs = [pl.BlockSpec((1, tc * 8, BRANCH_W), lambda d, hf, c: (hf, chunk(d, c), 0)),
                pl.BlockSpec((1, BRANCH_W, S5_W2), lambda d, hf, c: (d, 0, 0)),
                pl.BlockSpec((1, 1, S5_W2), lambda d, hf, c: (d, 0, 0))]
    args = [su2, bcat, a]
    if emit_y:
        in_specs.append(pl.BlockSpec((1, S5_W2, BRANCH_W), lambda d, hf, c: (d, 0, 0)))
        args.append(ccat)
    st_spec = pl.BlockSpec((1, 1, 8, S5_W2), lambda d, hf, c: (d, hf, 0, 0))
    in_specs.append(st_spec)
    args.append(init)
    st_shape = jax.ShapeDtypeStruct((2, 2, 8, S5_W2), F32)
    if emit_y:
        out_specs = [pl.BlockSpec((1, 1, tc * 8, BRANCH_W), lambda d, hf, c: (d, hf, chunk(d, c), 0)),
                     st_spec]
        out_shape = [jax.ShapeDtypeStruct((2, 2, lseg * 8, BRANCH_W), F32), st_shape]
    else:
        out_specs = [st_spec]
        out_shape = [st_shape]
    outs = pl.pallas_call(
        functools.partial(_s5_kernel, tc=tc, nch=nch, emit_y=emit_y),
        grid=(2, 2, nch),
        in_specs=in_specs, out_specs=out_specs, out_shape=out_shape,
        scratch_shapes=[pltpu.VMEM((tc * 8, S5_W2), F32), pltpu.VMEM((8, S5_W2), F32)],
        compiler_params=_cparams(("arbitrary", "arbitrary", "arbitrary")),
        name="s5_scan" if emit_y else "s5_endstate",
    )(*args)
    return outs


def _s5_carry_kernel(fin_ref, a_ref, init_ref, *, lseg):
    half = S5_W2 // 2
    for d in range(2):
        a = a_ref[d]
        pr, pi = a[:, :half], a[:, half:]
        for _ in range(int(math.log2(lseg))):
            pr, pi = pr * pr - pi * pi, 2.0 * pr * pi
        for hf in range(2):
            sr = jnp.zeros((1, half), F32)
            si = jnp.zeros((1, half), F32)
            order = range(8) if d == 0 else range(7, -1, -1)
            for s in order:
                init_ref[d, hf, s:s + 1, 0:half] = sr
                init_ref[d, hf, s:s + 1, half:S5_W2] = si
                fr = fin_ref[d, hf, s:s + 1, 0:half]
                fi = fin_ref[d, hf, s:s + 1, half:S5_W2]
                sr, si = pr * sr - pi * si + fr, pr * si + pi * sr + fi


def _s5_carry_call(geo, fin, a):
    return pl.pallas_call(
        functools.partial(_s5_carry_kernel, lseg=geo.lseg),
        out_shape=jax.ShapeDtypeStruct(fin.shape, F32),
        name="s5_carry",
    )(fin, a)


def _s5_mixer(geo, su, s5w):
    bcat, a, ccat = s5w
    zeros = jnp.zeros((2, 2, 8, S5_W2), F32)
    if geo.spb == 1:
        init = zeros
    else:
        assert geo.spb == 8
        (fin,) = _s5_call(geo, su, bcat, a, None, zeros, emit_y=False)
        init = _s5_carry_call(geo, fin, a)
    y, _ = _s5_call(geo, su, bcat, a, ccat, init, emit_y=True)
    return y.reshape(2, 2, geo.lseg, 8 * BRANCH_W)


def _s5_weights(a_re, a_im, log_dt, b_re, b_im, c_re, c_im):
    lam_re = jnp.minimum(a_re.astype(F32), -1e-4)
    lam_im = a_im.astype(F32)
    dt = jnp.exp(log_dt.astype(F32))[..., None]
    mag = jnp.exp(lam_re * dt)
    abar_re = mag * jnp.cos(lam_im * dt)
    abar_im = mag * jnp.sin(lam_im * dt)
    den = lam_re * lam_re + lam_im * lam_im
    f_re = ((abar_re - 1.0) * lam_re + abar_im * lam_im) / den
    f_im = (abar_im * lam_re - (abar_re - 1.0) * lam_im) / den
    br, bi = b_re.astype(F32), b_im.astype(F32)
    bb_re = f_re[..., None] * br - f_im[..., None] * bi
    bb_im = f_re[..., None] * bi + f_im[..., None] * br
    eye = jnp.eye(S5_G, dtype=F32)

    def bd_in(m):
        return jnp.einsum('dgph,gk->dghkp', m, eye).reshape(2, S5_G * S5_H, S5_G * S5_P)

    def bd_out(m):
        return jnp.einsum('dghp,gk->dgpkh', m, eye).reshape(2, S5_G * S5_P, S5_G * S5_H)

    bcat = jnp.concatenate([bd_in(bb_re), bd_in(bb_im)], axis=-1).astype(BF16)
    ccat = jnp.concatenate([bd_out(c_re.astype(F32)), -bd_out(c_im.astype(F32))], axis=1).astype(BF16)
    a = jnp.concatenate([abar_re.reshape(2, 1, -1), abar_im.reshape(2, 1, -1)], axis=-1)
    return bcat, a, ccat


def _ret_kernel(qf_ref, kf_ref, vf_ref, qb_ref, kb_ref, vb_ref, dec_ref, xi_ref, zeta_ref, cd_ref,
                of_ref, ob_ref, st_scr):
    i = pl.program_id(1)
    cr = qf_ref.shape[1]

    @pl.when(i == 0)
    def _():
        st_scr[...] = jnp.zeros(st_scr.shape, F32)

    lane = lax.broadcasted_iota(jnp.int32, (cr, BRANCH_W), 1) // HEAD_W
    blk = (lax.broadcasted_iota(jnp.int32, (BRANCH_W, BRANCH_W), 0) // HEAD_W
           == lax.broadcasted_iota(jnp.int32, (BRANCH_W, BRANCH_W), 1) // HEAD_W)
    dirs = ((qf_ref, kf_ref, vf_ref, of_ref), (qb_ref, kb_ref, vb_ref, ob_ref))
    for d, (q_ref, k_ref, v_ref, o_ref) in enumerate(dirs):
        q, k, v = q_ref[0], k_ref[0], v_ref[0]
        state = st_scr[d]
        o = jnp.dot(q, state.astype(BF16), preferred_element_type=F32) * xi_ref[d]
        for h in range(N_HEADS):
            qh = jnp.where(lane == h, q, jnp.zeros_like(q))
            vh = jnp.where(lane == h, v, jnp.zeros_like(v))
            s = lax.dot_general(qh, k, (((1,), (1,)), ((), ())), preferred_element_type=F32)
            s = s * dec_ref[d, h]
            o = o + jnp.dot(s.astype(BF16), vh, preferred_element_type=F32)
        o_ref[0] = o
        kz = (k.astype(F32) * zeta_ref[d]).T.astype(BF16)
        kv = jnp.dot(kz, v, preferred_element_type=F32)
        st_scr[d] = state * cd_ref[d] + jnp.where(blk, kv, 0.0)


def _ret_call(rq, rk, rv, tabs):
    b, l, _ = rq.shape
    cr = tabs[0].shape[-1]
    n = l // cr
    fwd = pl.BlockSpec((1, cr, BRANCH_W), lambda bb, i: (bb, i, 0))
    bwd = pl.BlockSpec((1, cr, BRANCH_W), lambda bb, i: (bb, n - 1 - i, 0))
    shp = jax.ShapeDtypeStruct((b, l, BRANCH_W), F32)
    return pl.pallas_call(
        _ret_kernel,
        grid=(b, n),
        in_specs=[fwd, fwd, fwd, bwd, bwd, bwd] + [_const_spec(t.shape) for t in tabs],
        out_specs=[fwd, bwd],
        out_shape=[shp, shp],
        scratch_shapes=[pltpu.VMEM((2, BRANCH_W, BRANCH_W), F32)],
        compiler_params=_cparams(("parallel", "arbitrary")),
        name="retention",
    )(rq, rk, rv, rq, rk, rv, *tabs)


def _ret_tables(ret_log_decay, cr):
    lg = -jnp.exp(ret_log_decay.astype(F32))
    idx = jnp.arange(cr, dtype=F32)
    rel = idx[:, None] - idx[None, :]
    m_f = rel >= 0
    m_b = rel < 0
    dec_f = jnp.where(m_f[None], jnp.exp(lg[0][:, None, None] * jnp.where(m_f, rel, 0.0)[None]), 0.0)
    dec_b = jnp.where(m_b[None], jnp.exp(lg[1][:, None, None] * jnp.where(m_b, -rel, 0.0)[None]), 0.0)
    dec = jnp.stack([dec_f, dec_b])
    lane_lg = jnp.repeat(lg, HEAD_W, axis=1)
    xi = jnp.stack([jnp.exp(lane_lg[0][None, :] * (idx + 1.0)[:, None]),
                    jnp.exp(lane_lg[1][None, :] * (cr - idx)[:, None])])
    zeta = jnp.stack([jnp.exp(lane_lg[0][None, :] * (cr - 1.0 - idx)[:, None]),
                      jnp.exp(lane_lg[1][None, :] * idx[:, None])])
    cd = jnp.exp(lane_lg * cr)[:, None, :]
    return dec, xi, zeta, cd


def _merge_kernel(x_ref, g_ref, sc_ref, sh_ref, g1_ref,
                  oaT_ref, ob_ref, su_ref, yf_ref, yb_ref, rf_ref, rb_ref, rg_ref,
                  s5d_ref, gluw_ref, glub_ref, rng_ref, ones_ref,
                  wg_ref, wb_ref, wo_ref, o_ref):
    x = x_ref[0]
    h = _norm_mod(x, g_ref[...], sc_ref[0], sh_ref[0]).astype(BF16)

    o_a = oaT_ref[0].T
    o_b = ob_ref[0]

    y = s5d_ref[...] * su_ref[0] + yf_ref[0, 0] + yb_ref[0, 0]
    y = jax.nn.gelu(y)
    o_c = y * jax.nn.sigmoid(jnp.dot(y.astype(BF16), gluw_ref[...], preferred_element_type=F32)
                             + glub_ref[...])

    z = rf_ref[0] + rb_ref[0]
    z2 = z * z
    hi = z2.astype(BF16)
    lo = (z2 - hi.astype(F32)).astype(BF16)
    ss = (jnp.dot(hi, ones_ref[...], preferred_element_type=F32)
          + jnp.dot(lo, ones_ref[...], preferred_element_type=F32))
    rg = rg_ref[0]
    o_d = z * lax.rsqrt(ss * (1.0 / HEAD_W) + EPS) * rng_ref[...] * (rg * jax.nn.sigmoid(rg))

    merged = None
    for i, br in enumerate((o_a, o_b, o_c, o_d)):
        gate = jax.nn.sigmoid(jnp.dot(h, wg_ref[i], preferred_element_type=F32))
        t = gate * jnp.dot(br.astype(BF16), wb_ref[i], preferred_element_type=F32)
        merged = t if merged is None else merged + t
    out = jnp.dot(merged.astype(BF16), wo_ref[...], preferred_element_type=F32)
    o_ref[0] = x + g1_ref[0] * out


def _merge_call(geo, x, g, sc, sh, g1, oaT, ob, su, y, rf, rb, rg, mw):
    b, l, d = x.shape
    tm = geo.tm
    tok = pl.BlockSpec((1, tm, BRANCH_W), lambda bb, i: (bb, i, 0))
    xs = pl.BlockSpec((1, tm, d), lambda bb, i: (bb, i, 0))
    vec = pl.BlockSpec((1, 1, d), lambda bb, i: (bb, 0, 0))
    su_spec = pl.BlockSpec((1, tm, BRANCH_W), lambda bb, i: geo.scan_index(bb, i))

    def y_spec(dd):
        return pl.BlockSpec((1, 1, tm, BRANCH_W), lambda bb, i: (dd,) + geo.scan_index(bb, i))

    s5_d, glu_w, glu_b, ret_g, ones_bd, wg, wb, wo = mw
    return pl.pallas_call(
        _merge_kernel,
        grid=(b, geo.nt),
        in_specs=[xs, _const_spec((1, d)), vec, vec, vec,
                  pl.BlockSpec((1, BRANCH_W, tm), lambda bb, i: (bb, 0, i)), tok,
                  su_spec, y_spec(0), y_spec(1), tok, tok, tok,
                  _const_spec(s5_d.shape), _const_spec(glu_w.shape), _const_spec(glu_b.shape),
                  _const_spec(ret_g.shape), _const_spec(ones_bd.shape),
                  _const_spec(wg.shape), _const_spec(wb.shape), _const_spec(wo.shape)],
        out_specs=xs,
        out_shape=jax.ShapeDtypeStruct(x.shape, F32),
        compiler_params=_cparams(("parallel", "parallel")),
        name="merge",
    )(x, g, sc, sh, g1, oaT, ob, su, y, y, rf, rb, rg, s5_d, glu_w, glu_b, ret_g, ones_bd, wg, wb, wo)


HALO = 8


def _ffn_kernel(xp_ref, x_ref, xn_ref, g_ref, sc_ref, sh_ref, g2_ref,
                wu_ref, wgt_ref, cw_ref, cb_ref, wd_ref, fg_ref, o_ref, a_scr, *, nt, final):
    i = pl.program_id(1)
    tm = x_ref.shape[1]
    g, sc, sh = g_ref[...], sc_ref[0], sh_ref[0]
    x = x_ref[0]
    h = _norm_mod(x, g, sc, sh)
    hp = _norm_mod(xp_ref[0], g, sc, sh) * (i > 0).astype(F32)
    hn = _norm_mod(xn_ref[0], g, sc, sh) * (i < nt - 1).astype(F32)
    h_ext = jnp.concatenate([hp, h, hn], axis=0).astype(BF16)
    a_scr[...] = jnp.dot(h_ext, wu_ref[...], preferred_element_type=F32)
    cw = cw_ref[...]
    a = (a_scr[pl.ds(HALO - 1, tm), :] * cw[0:1] + a_scr[pl.ds(HALO, tm), :] * cw[1:2]
         + a_scr[pl.ds(HALO + 1, tm), :] * cw[2:3] + cb_ref[...])
    gate = jnp.dot(h.astype(BF16), wgt_ref[...], preferred_element_type=F32)
    act = (jax.nn.gelu(a) * gate).astype(BF16)
    y = x + g2_ref[0] * jnp.dot(act, wd_ref[...], preferred_element_type=F32)
    if final:
        ms = jnp.mean(y * y, axis=-1, keepdims=True)
        y = y * lax.rsqrt(ms + EPS) * fg_ref[...]
    o_ref[0] = y


def _ffn_call(geo, x, g, sc, sh, g2, fw, final_g, final):
    b, l, d = x.shape
    tm = min(geo.tm, 256)
    nt = l // tm
    r = tm // HALO
    nh = l // HALO
    xs = pl.BlockSpec((1, tm, d), lambda bb, i: (bb, i, 0))
    prev = pl.BlockSpec((1, HALO, d), lambda bb, i: (bb, jnp.maximum(i * r - 1, 0), 0))
    nxt = pl.BlockSpec((1, HALO, d), lambda bb, i: (bb, jnp.minimum((i + 1) * r, nh - 1), 0))
    vec = pl.BlockSpec((1, 1, d), lambda bb, i: (bb, 0, 0))
    wu, wgt, cw, cb, wd = fw
    once = pl.Buffered(1)

    def wspec(shape):
        return pl.BlockSpec(shape, lambda *_: (0,) * len(shape), pipeline_mode=once)

    return pl.pallas_call(
        functools.partial(_ffn_kernel, nt=nt, final=final),
        grid=(b, nt),
        in_specs=[prev, xs, nxt, _const_spec((1, d)), vec, vec, vec,
                  wspec(wu.shape), wspec(wgt.shape), _const_spec(cw.shape), _const_spec(cb.shape),
                  wspec(wd.shape), _const_spec((1, d))],
        out_specs=xs,
        out_shape=jax.ShapeDtypeStruct(x.shape, F32),
        scratch_shapes=[pltpu.VMEM((tm + 2 * HALO, D_FF), F32)],
        compiler_params=_cparams(("parallel", "parallel")),
        name="conv_ffn",
    )(x, x, x, g, sc, sh, g2, wu, wgt, cw, cb, wd, final_g)


RET_CHUNK = 256


def _layer_weights(l, p):
    d = p['w_in'].shape[1]
    ones_bd = jnp.asarray(np.kron(np.eye(N_HEADS), np.ones((HEAD_W, HEAD_W))), BF16)
    return dict(
        norm1_g=p['norm1_g'][l].reshape(1, d),
        norm2_g=p['norm2_g'][l].reshape(1, d),
        w_in=p['w_in'][l].astype(BF16),
        da_lambda=p['da_lambda'][l].astype(F32),
        da_subln_g=p['da_subln_g'][l].astype(F32),
        na_bias=_na_bias_table(p['na_rpb'][l]),
        s5=_s5_weights(p['s5_a_re'][l], p['s5_a_im'][l], p['s5_log_dt'][l], p['s5_b_re'][l],
                       p['s5_b_im'][l], p['s5_c_re'][l], p['s5_c_im'][l]),
        ret=_ret_tables(p['ret_log_decay'][l], RET_CHUNK),
        merge=(p['s5_d'][l].reshape(1, BRANCH_W).astype(F32),
               p['s5_glu_w'][l].astype(BF16),
               p['s5_glu_b'][l].reshape(1, BRANCH_W).astype(F32),
               jnp.tile(p['ret_norm_g'][l].astype(F32), N_HEADS).reshape(1, BRANCH_W),
               ones_bd,
               p['w_branch_gate'][l].astype(BF16),
               p['w_branch'][l].astype(BF16),
               p['w_out'][l].astype(BF16)),
        ffn=(p['ffn_w_up'][l].astype(BF16),
             p['ffn_w_gate'][l].astype(BF16),
             p['ffn_conv_w'][l].reshape(3, D_FF).astype(F32),
             p['ffn_conv_b'][l].reshape(1, D_FF).astype(F32),
             p['ffn_w_down'][l].astype(BF16)),
    )


def _ret_tabs_for(geo, ret_log_decay_l):
    return _ret_tables(ret_log_decay_l, min(RET_CHUNK, geo.l))


def _layer(geo, x, mod, lw, lam_init, rope, final_g, final):
    b, l, d = x.shape
    sh1, sc1, g1, sh2, sc2, g2 = [mod[:, k * d:(k + 1) * d].reshape(b, 1, d) for k in range(6)]
    (qT, k, vT, nq, nk, nv, su, rq, rk, rv, rg) = _proj_call(
        geo, x, lw['norm1_g'], sc1, sh1, lw['w_in'], rope)
    oaT = _da_call(qT, k, vT, lw['da_lambda'], lw['da_subln_g'], lam_init)
    ob = _na_call(geo, nq, nk, nv, lw['na_bias'])
    y = _s5_mixer(geo, su, lw['s5'])
    rf, rb = _ret_call(rq, rk, rv, lw['ret'])
    x = _merge_call(geo, x, lw['norm1_g'], sc1, sh1, g1, oaT, ob, su, y, rf, rb, rg, lw['merge'])
    return _ffn_call(geo, x, lw['norm2_g'], sc2, sh2, g2, lw['ffn'], final_g, final)


def kernel(x_prompt, x_sample, c_prompt, c_sample, norm1_g, norm2_g, ada_w, ada_b, w_in, da_lambda, da_subln_g, na_rpb, s5_a_re, s5_a_im, s5_log_dt, s5_b_re, s5_b_im, s5_c_re, s5_c_im, s5_d, s5_glu_w, s5_glu_b, ret_log_decay, ret_norm_g, w_branch, w_branch_gate, w_out, ffn_w_up, ffn_w_gate, ffn_conv_w, ffn_conv_b, ffn_w_down, final_norm_g):
    p = dict(norm1_g=norm1_g, norm2_g=norm2_g, w_in=w_in, da_lambda=da_lambda,
             da_subln_g=da_subln_g, na_rpb=na_rpb, s5_a_re=s5_a_re, s5_a_im=s5_a_im,
             s5_log_dt=s5_log_dt, s5_b_re=s5_b_re, s5_b_im=s5_b_im, s5_c_re=s5_c_re,
             s5_c_im=s5_c_im, s5_d=s5_d, s5_glu_w=s5_glu_w, s5_glu_b=s5_glu_b,
             ret_log_decay=ret_log_decay, ret_norm_g=ret_norm_g, w_branch=w_branch,
             w_branch_gate=w_branch_gate, w_out=w_out, ffn_w_up=ffn_w_up, ffn_w_gate=ffn_w_gate,
             ffn_conv_w=ffn_conv_w, ffn_conv_b=ffn_conv_b, ffn_w_down=ffn_w_down)
    depth = w_in.shape[0]
    d = w_in.shape[1]
    nbp = c_prompt.shape[0]
    mod = _mod_call(jnp.concatenate([c_prompt, c_sample], axis=0), ada_w, ada_b)
    lws = [_layer_weights(l, p) for l in range(depth)]
    final_g = final_norm_g.reshape(1, d).astype(F32)
    outs = []
    for x, lo in ((x_prompt, 0), (x_sample, nbp)):
        b, l, _ = x.shape
        geo = _Geom(b, l)
        rope = _rope_tables(l)
        for li in range(depth):
            lam_init = 0.8 - 0.6 * math.exp(-0.3 * li)
            x = _layer(geo, x, mod[li, lo:lo + b], lws[li], lam_init, rope, final_g,
                       final=(li == depth - 1))
        outs.append(x)
    return tuple(outs)
```

```python
import functools
import math

import numpy as np
import jax
import jax.numpy as jnp
from jax import lax
from jax.experimental import pallas as pl
from jax.experimental.pallas import tpu as pltpu

F32 = jnp.float32
BF16 = jnp.bfloat16

EPS = 1e-6
ROPE_THETA = 500000.0
BRANCH_W = 256
N_SEG = 11
DA_DIM = 32
HEAD_W = 64
N_HEADS = 4
GRID_W = 64
NA_ROWS = 8
NA_COLS = 16
S5_G = 16
S5_P = 64
S5_H = 16
S5_NSEQ = 16
D_FF = 2816
NEG = -1e30
LOG2E = 1.4426950408889634

VMEM_LIMIT = 56 * 1024 * 1024


def _cparams(sem):
    return pltpu.CompilerParams(dimension_semantics=sem, vmem_limit_bytes=VMEM_LIMIT)


def _const_spec(shape):
    nd = len(shape)
    return pl.BlockSpec(shape, lambda *_: (0,) * nd)


def _norm_mod(x, g, sc, sh):
    ms = jnp.mean(x * x, axis=-1, keepdims=True)
    y = x * lax.rsqrt(ms + EPS) * g
    return y * (1.0 + sc) + sh


def _mod_kernel(c_ref, w_ref, b_ref, o_ref):
    c = c_ref[...]
    s = c * jax.nn.sigmoid(c)
    o_ref[0] = jnp.dot(s.astype(BF16), w_ref[0].astype(BF16),
                       preferred_element_type=F32) + b_ref[0]


def _mod_call(c_all, ada_w, ada_b):
    depth, d, n6 = ada_w.shape
    nb = c_all.shape[0]
    tn = 1536
    return pl.pallas_call(
        _mod_kernel,
        grid=(depth, n6 // tn),
        in_specs=[pl.BlockSpec((nb, d), lambda l, j: (0, 0)),
                  pl.BlockSpec((1, d, tn), lambda l, j: (l, 0, j)),
                  pl.BlockSpec((1, 1, tn), lambda l, j: (l, 0, j))],
        out_specs=pl.BlockSpec((1, nb, tn), lambda l, j: (l, 0, j)),
        out_shape=jax.ShapeDtypeStruct((depth, nb, n6), F32),
        compiler_params=_cparams(("arbitrary", "arbitrary")),
        name="adaln_mod",
    )(c_all, ada_w, ada_b.reshape(depth, 1, n6))


class _Geom:
    def __init__(self, b, l):
        assert S5_NSEQ % b == 0
        self.b, self.l = b, l
        self.spb = S5_NSEQ // b
        assert l % self.spb == 0
        self.lseg = l // self.spb
        assert self.lseg & (self.lseg - 1) == 0
        self.tm = min(512, self.lseg)
        assert self.lseg % self.tm == 0
        self.tps = self.lseg // self.tm
        self.nt = l // self.tm
        self.rows = l // GRID_W
        assert self.rows >= NA_ROWS and l % (NA_ROWS * GRID_W) == 0

    def scan_index(self, b, i):
        n = b * self.spb + i // self.tps
        return n // 8, i % self.tps, n % 8


def _proj_kernel(x_ref, g_ref, sc_ref, sh_ref, w_ref, cos_ref, sina_ref, sinb_ref,
                 qT_ref, k_ref, vT_ref, nq_ref, nk_ref, nv_ref, su_ref,
                 rq_ref, rk_ref, rv_ref, rg_ref):
    h = _norm_mod(x_ref[0], g_ref[...], sc_ref[0], sh_ref[0]).astype(BF16)

    def seg(j):
        return jnp.dot(h, w_ref[:, j * BRANCH_W:(j + 1) * BRANCH_W],
                       preferred_element_type=F32)

    cos, sina, sinb = cos_ref[...], sina_ref[...], sinb_ref[...]

    def rot(p):
        halves = []
        for s in range(2):
            ph = p[:, s * 128:(s + 1) * 128]
            halves.append(ph * cos + pltpu.roll(ph, 124, 1) * sina + pltpu.roll(ph, 4, 1) * sinb)
        return jnp.concatenate(halves, axis=1)

    qscale = DA_DIM ** -0.5 * LOG2E
    qT_ref[0] = (rot(seg(0)) * qscale).T.astype(BF16)
    k_ref[0] = rot(seg(1)).astype(BF16)
    vT_ref[0] = seg(2).T.astype(BF16)
    nq_ref[0] = (seg(3) * HEAD_W ** -0.5).astype(BF16)
    nk_ref[0] = seg(4).astype(BF16)
    nv_ref[0] = seg(5).astype(BF16)
    su_ref[0] = seg(6)
    rq_ref[0] = seg(7).astype(BF16)
    rk_ref[0] = (seg(8) * HEAD_W ** -0.5).astype(BF16)
    rv_ref[0] = seg(9).astype(BF16)
    rg_ref[0] = seg(10)


def _proj_call(geo, x, g, sc, sh, w_in, rope):
    b, l, d = x.shape
    tm = geo.tm
    tok = pl.BlockSpec((1, tm, BRANCH_W), lambda bb, i: (bb, i, 0))
    tokT = pl.BlockSpec((1, BRANCH_W, tm), lambda bb, i: (bb, 0, i))
    vec = pl.BlockSpec((1, 1, d), lambda bb, i: (bb, 0, 0))
    tab = pl.BlockSpec((tm, 128), lambda bb, i: (i, 0))
    su_spec = pl.BlockSpec((1, tm, BRANCH_W), lambda bb, i: geo.scan_index(bb, i))
    bl = jax.ShapeDtypeStruct((b, l, BRANCH_W), BF16)
    blT = jax.ShapeDtypeStruct((b, BRANCH_W, l), BF16)
    blf = jax.ShapeDtypeStruct((b, l, BRANCH_W), F32)
    su_shape = jax.ShapeDtypeStruct((2, geo.lseg, 8 * BRANCH_W), F32)
    return pl.pallas_call(
        _proj_kernel,
        grid=(b, geo.nt),
        in_specs=[pl.BlockSpec((1, tm, d), lambda bb, i: (bb, i, 0)),
                  _const_spec((1, d)), vec, vec,
                  _const_spec((d, N_SEG * BRANCH_W)), tab, tab, tab],
        out_specs=[tokT, tok, tokT, tok, tok, tok, su_spec, tok, tok, tok, tok],
        out_shape=[blT, bl, blT, bl, bl, bl, su_shape, bl, bl, bl, blf],
        compiler_params=_cparams(("parallel", "parallel")),
        name="proj",
    )(x, g, sc, sh, w_in, *rope)


def _rope_tables(l):
    half = DA_DIM // 4 // 2
    inv_freq = jnp.power(jnp.float32(ROPE_THETA), -jnp.arange(half, dtype=F32) / half)
    ang = jnp.arange(l).astype(F32)[:, None] * inv_freq[None, :]
    cos, sin = jnp.cos(ang), jnp.sin(ang)
    zeros = jnp.zeros((l, half), F32)
    rest = DA_DIM - 2 * half

    def lanes(first, second, fill):
        comp = jnp.concatenate([first, second, jnp.full((l, rest), fill, F32)], axis=1)
        return jnp.tile(comp, (1, 128 // DA_DIM))

    return lanes(cos, cos, 1.0), lanes(-sin, zeros, 0.0), lanes(zeros, sin, 0.0)


DA_TQ = 1024
DA_TK = 1024
DA_SEED = 128
DA_SLACK = 60.0


def _da_kernel(lam_ref, g_ref, qT_ref, k_ref, vT_ref, o_ref,
               qm_scr, m_scr, l_scr, acc_scr, cur_ref, *, lam_init, nkv):
    kv = pl.program_id(2)
    n_sets = 2 * N_HEADS

    @pl.when(kv == 0)
    def _():
        qT = qT_ref[0]
        row = lax.broadcasted_iota(jnp.int32, qT.shape, 0)
        k_seed = k_ref[0, 0:DA_SEED, :]
        for hc in range(n_sets):
            keep = (row >= hc * DA_DIM) & (row < (hc + 1) * DA_DIM)
            qm = jnp.where(keep, qT, jnp.zeros_like(qT))
            qm_scr[hc] = qm
            m_scr[hc] = jnp.max(jnp.dot(k_seed, qm, preferred_element_type=F32),
                                axis=0, keepdims=True)
        l_scr[...] = jnp.zeros(l_scr.shape, F32)
        acc_scr[...] = jnp.zeros(acc_scr.shape, F32)
        cur_ref[0] = 0

    cur = cur_ref[0]
    nxt = 1 - cur
    kb = k_ref[0]

    worst = None
    for hc in range(n_sets):
        h = hc // 2
        m_ref = m_scr[hc]
        s_t = jnp.dot(kb, qm_scr[hc], preferred_element_type=F32)
        over = jnp.max(s_t, axis=0, keepdims=True) - m_ref
        worst = over if worst is None else jnp.maximum(worst, over)
        p = jnp.exp2(s_t - m_ref)
        l_scr[nxt, hc] = l_scr[cur, hc] + jnp.sum(p, axis=0, keepdims=True)
        acc_scr[nxt, hc] = acc_scr[cur, hc] + jnp.dot(
            vT_ref[0, h * HEAD_W:(h + 1) * HEAD_W, :], p.astype(BF16), preferred_element_type=F32)
    within = jnp.max(worst) <= DA_SLACK

    @pl.when(within)
    def _():
        cur_ref[0] = nxt

    @pl.when(jnp.logical_not(within))
    def _():
        for hc in range(n_sets):
            h = hc // 2
            s_t = jnp.dot(kb, qm_scr[hc], preferred_element_type=F32)
            m_old = m_scr[hc]
            m_new = jnp.maximum(m_old, jnp.max(s_t, axis=0, keepdims=True))
            p = jnp.exp2(s_t - m_new)
            alpha = jnp.exp2(m_old - m_new)
            l_scr[cur, hc] = alpha * l_scr[cur, hc] + jnp.sum(p, axis=0, keepdims=True)
            pv = jnp.dot(vT_ref[0, h * HEAD_W:(h + 1) * HEAD_W, :], p.astype(BF16),
                         preferred_element_type=F32)
            acc_scr[cur, hc] = alpha * acc_scr[cur, hc] + pv
            m_scr[hc] = m_new

    @pl.when(kv == nkv - 1)
    def _():
        fin = cur_ref[0]
        lp = lam_ref[...]
        lam = (jnp.exp(jnp.sum(lp[0:1] * lp[1:2], axis=-1, keepdims=True))
               - jnp.exp(jnp.sum(lp[2:3] * lp[3:4], axis=-1, keepdims=True)) + lam_init)
        for h in range(N_HEADS):
            o1 = acc_scr[fin, 2 * h] / l_scr[fin, 2 * h]
            o2 = acc_scr[fin, 2 * h + 1] / l_scr[fin, 2 * h + 1]
            o = o1 - lam * o2
            ms = jnp.mean(o * o, axis=0, keepdims=True)
            y = o * lax.rsqrt(ms + EPS) * g_ref[h * HEAD_W:(h + 1) * HEAD_W, :]
            o_ref[0, h * HEAD_W:(h + 1) * HEAD_W, :] = y * (1.0 - lam_init)


def _da_call(qT, k, vT, da_lambda, subln_g, lam_init):
    b, _, l = qT.shape
    tq = min(DA_TQ, l)
    tk = min(DA_TK, l)
    nkv = l // tk
    g_col = jnp.tile(subln_g, N_HEADS).reshape(BRANCH_W, 1)
    return pl.pallas_call(
        functools.partial(_da_kernel, lam_init=lam_init, nkv=nkv),
        grid=(b, l // tq, nkv),
        in_specs=[_const_spec((4, DA_DIM)), _const_spec((BRANCH_W, 1)),
                  pl.BlockSpec((1, BRANCH_W, tq), lambda bb, i, j: (bb, 0, i)),
                  pl.BlockSpec((1, tk, BRANCH_W), lambda bb, i, j: (bb, j, 0)),
                  pl.BlockSpec((1, BRANCH_W, tk), lambda bb, i, j: (bb, 0, j))],
        out_specs=pl.BlockSpec((1, BRANCH_W, tq), lambda bb, i, j: (bb, 0, i)),
        out_shape=jax.ShapeDtypeStruct((b, BRANCH_W, l), F32),
        scratch_shapes=[pltpu.VMEM((2 * N_HEADS, BRANCH_W, tq), BF16),
                        pltpu.VMEM((2 * N_HEADS, 1, tq), F32),
                        pltpu.VMEM((2, 2 * N_HEADS, 1, tq), F32),
                        pltpu.VMEM((2, 2 * N_HEADS, HEAD_W, tq), F32),
                        pltpu.SMEM((1,), jnp.int32)],
        compiler_params=_cparams(("parallel", "parallel", "arbitrary")),
        name="diff_attn",
    )(da_lambda, g_col, qT, k, vT)


NA_RB = 8
NA_WIN = NA_ROWS * GRID_W


def _na_kernel(q_ref, k_ref, v_ref, bias_ref, o_ref, *, rows):
    i = pl.program_id(1)
    lane_q = lax.broadcasted_iota(jnp.int32, (GRID_W, BRANCH_W), 1) // HEAD_W
    for j in range(NA_RB):
        r = i * NA_RB + j
        rs = jnp.clip(r - NA_ROWS // 2, 0, rows - NA_ROWS)
        start = pl.multiple_of(rs * GRID_W, GRID_W)
        kw = k_ref[0, pl.ds(start, NA_WIN), :]
        vw = v_ref[0, pl.ds(start, NA_WIN), :]
        q = q_ref[0, j * GRID_W:(j + 1) * GRID_W, :]
        qm = jnp.concatenate(
            [jnp.where(lane_q == h, q, jnp.zeros_like(q)) for h in range(N_HEADS)], axis=0)
        s = lax.dot_general(qm, kw, (((1,), (1,)), ((), ())), preferred_element_type=F32)
        s = s + bias_ref[r - rs]
        m = jnp.max(s, axis=-1, keepdims=True)
        e = jnp.exp(s - m)
        den = jnp.sum(e, axis=-1, keepdims=True)
        o_all = jnp.dot(e.astype(BF16), vw, preferred_element_type=F32) / den
        o = jnp.zeros((GRID_W, BRANCH_W), F32)
        for h in range(N_HEADS):
            o = o + jnp.where(lane_q == h, o_all[h * GRID_W:(h + 1) * GRID_W, :], 0.0)
        o_ref[0, j * GRID_W:(j + 1) * GRID_W, :] = o


def _na_call(geo, nq, nk, nv, bias):
    b, l, _ = nq.shape
    blk = NA_RB * GRID_W
    full = pl.BlockSpec((1, l, BRANCH_W), lambda bb, i: (bb, 0, 0))
    return pl.pallas_call(
        functools.partial(_na_kernel, rows=geo.rows),
        grid=(b, l // blk),
        in_specs=[pl.BlockSpec((1, blk, BRANCH_W), lambda bb, i: (bb, i, 0)), full, full,
                  _const_spec(bias.shape)],
        out_specs=pl.BlockSpec((1, blk, BRANCH_W), lambda bb, i: (bb, i, 0)),
        out_shape=jax.ShapeDtypeStruct((b, l, BRANCH_W), F32),
        compiler_params=_cparams(("parallel", "arbitrary")),
        name="nbr_attn",
    )(nq, nk, nv, bias)


def _na_bias_table(rpb):
    w = np.arange(GRID_W)
    cs = np.clip(w - NA_COLS // 2, 0, GRID_W - NA_COLS)
    kc = np.arange(GRID_W)
    allowed = (kc[None, :] >= cs[:, None]) & (kc[None, :] < cs[:, None] + NA_COLS)
    dcol = np.clip(kc[None, :] - w[:, None] + (NA_COLS - 1), 0, 2 * NA_COLS - 2)
    delta = np.arange(NA_ROWS)
    wi = np.arange(NA_ROWS)
    drow = wi[None, :] - delta[:, None] + (NA_ROWS - 1)
    sel_r = jnp.asarray(drow[:, :, None] == np.arange(2 * NA_ROWS - 1), F32)
    sel_c = jnp.asarray(dcol[:, :, None] == np.arange(2 * NA_COLS - 1), F32)
    t = jnp.einsum('xia,hab,wkb->xhwik', sel_r, rpb.astype(F32), sel_c,
                   precision=lax.Precision.HIGHEST)
    t = jnp.where(jnp.asarray(allowed)[None, None, :, None, :], t, NEG)
    return t.reshape(NA_ROWS, N_HEADS * GRID_W, NA_WIN)


S5_W2 = 2 * S5_G * S5_P


def _s5_kernel(*refs, tc, nch, emit_y):
    if emit_y:
        su_ref, b_ref, a_ref, c_ref, init_ref, y_ref, fin_ref, bu_scr, st_scr = refs
    else:
        su_ref, b_ref, a_ref, init_ref, fin_ref, bu_scr, st_scr = refs
    d = pl.program_id(0)
    c = pl.program_id(2)
    half = S5_W2 // 2

    @pl.when(c == 0)
    def _():
        st_scr[...] = init_ref[0, 0]

    rh = tc * 4
    for r in range(2):
        bu_scr[r * rh:(r + 1) * rh, :] = jnp.dot(su_ref[0, r * rh:(r + 1) * rh, :].astype(BF16),
                                                 b_ref[0], preferred_element_type=F32)
    a = a_ref[0]
    ar = jnp.broadcast_to(a[:, :half], (8, half))
    ai = jnp.broadcast_to(a[:, half:], (8, half))

    def body(t, carry):
        sr, si = carry
        tt = jnp.where(d == 0, t, tc - 1 - t)
        off = pl.multiple_of(tt * 8, 8)
        br = bu_scr[pl.ds(off, 8), 0:half]
        bi = bu_scr[pl.ds(off, 8), half:S5_W2]
        nr = ar * sr - ai * si + br
        ni = ar * si + ai * sr + bi
        if emit_y:
            bu_scr[pl.ds(off, 8), 0:half] = nr
            bu_scr[pl.ds(off, 8), half:S5_W2] = ni
        return nr, ni

    sr, si = lax.fori_loop(0, tc, body, (st_scr[:, 0:half], st_scr[:, half:S5_W2]), unroll=4)
    st_scr[:, 0:half] = sr
    st_scr[:, half:S5_W2] = si
    if emit_y:
        for r in range(2):
            y_ref[0, 0, r * rh:(r + 1) * rh, :] = jnp.dot(
                bu_scr[r * rh:(r + 1) * rh, :].astype(BF16), c_ref[0], preferred_element_type=F32)

    @pl.when(c == nch - 1)
    def _():
        fin_ref[0, 0] = st_scr[...]


def _s5_call(geo, su, bcat, a, ccat, init, emit_y):
    lseg = geo.lseg
    tc = min(128, lseg)
    nch = lseg // tc
    su2 = su.reshape(2, lseg * 8, BRANCH_W)

    def chunk(d, c):
        return jnp.where(d == 0, c, nch - 1 - c)

    in_specs = [pl.BlockSpec((1, tc * 8, BRANCH_W), lambda d, hf, c: (hf, chunk(d, c), 0)),
                pl.BlockSpec((1, BRANCH_W, S5_W2), lambda d, hf, c: (d, 0, 0)),
                pl.BlockSpec((1, 1, S5_W2), lambda d, hf, c: (d, 0, 0))]
    args = [su2, bcat, a]
    if emit_y:
        in_specs.append(pl.BlockSpec((1, S5_W2, BRANCH_W), lambda d, hf, c: (d, 0, 0)))
        args.append(ccat)
    st_spec = pl.BlockSpec((1, 1, 8, S5_W2), lambda d, hf, c: (d, hf, 0, 0))
    in_specs.append(st_spec)
    args.append(init)
    st_shape = jax.ShapeDtypeStruct((2, 2, 8, S5_W2), F32)
    if emit_y:
        out_specs = [pl.BlockSpec((1, 1, tc * 8, BRANCH_W), lambda d, hf, c: (d, hf, chunk(d, c), 0)),
                     st_spec]
        out_shape = [jax.ShapeDtypeStruct((2, 2, lseg * 8, BRANCH_W), F32), st_shape]
    else:
        out_specs = [st_spec]
        out_shape = [st_shape]
    outs = pl.pallas_call(
        functools.partial(_s5_kernel, tc=tc, nch=nch, emit_y=emit_y),
        grid=(2, 2, nch),
        in_specs=in_specs, out_specs=out_specs, out_shape=out_shape,
        scratch_shapes=[pltpu.VMEM((tc * 8, S5_W2), F32), pltpu.VMEM((8, S5_W2), F32)],
        compiler_params=_cparams(("arbitrary", "arbitrary", "arbitrary")),
        name="s5_scan" if emit_y else "s5_endstate",
    )(*args)
    return outs


def _s5_carry_kernel(fin_ref, a_ref, init_ref, *, lseg):
    half = S5_W2 // 2
    for d in range(2):
        a = a_ref[d]
        pr, pi = a[:, :half], a[:, half:]
        for _ in range(int(math.log2(lseg))):
            pr, pi = pr * pr - pi * pi, 2.0 * pr * pi
        for hf in range(2):
            sr = jnp.zeros((1, half), F32)
            si = jnp.zeros((1, half), F32)
            order = range(8) if d == 0 else range(7, -1, -1)
            for s in order:
                init_ref[d, hf, s:s + 1, 0:half] = sr
                init_ref[d, hf, s:s + 1, half:S5_W2] = si
                fr = fin_ref[d, hf, s:s + 1, 0:half]
                fi = fin_ref[d, hf, s:s + 1, half:S5_W2]
                sr, si = pr * sr - pi * si + fr, pr * si + pi * sr + fi


def _s5_carry_call(geo, fin, a):
    return pl.pallas_call(
        functools.partial(_s5_carry_kernel, lseg=geo.lseg),
        out_shape=jax.ShapeDtypeStruct(fin.shape, F32),
        name="s5_carry",
    )(fin, a)


def _s5_mixer(geo, su, s5w):
    bcat, a, ccat = s5w
    zeros = jnp.zeros((2, 2, 8, S5_W2), F32)
    if geo.spb == 1:
        init = zeros
    else:
        assert geo.spb == 8
        (fin,) = _s5_call(geo, su, bcat, a, None, zeros, emit_y=False)
        init = _s5_carry_call(geo, fin, a)
    y, _ = _s5_call(geo, su, bcat, a, ccat, init, emit_y=True)
    return y.reshape(2, 2, geo.lseg, 8 * BRANCH_W)


def _s5_weights(a_re, a_im, log_dt, b_re, b_im, c_re, c_im):
    lam_re = jnp.minimum(a_re.astype(F32), -1e-4)
    lam_im = a_im.astype(F32)
    dt = jnp.exp(log_dt.astype(F32))[..., None]
    mag = jnp.exp(lam_re * dt)
    abar_re = mag * jnp.cos(lam_im * dt)
    abar_im = mag * jnp.sin(lam_im * dt)
    den = lam_re * lam_re + lam_im * lam_im
    f_re = ((abar_re - 1.0) * lam_re + abar_im * lam_im) / den
    f_im = (abar_im * lam_re - (abar_re - 1.0) * lam_im) / den
    br, bi = b_re.astype(F32), b_im.astype(F32)
    bb_re = f_re[..., None] * br - f_im[..., None] * bi
    bb_im = f_re[..., None] * bi + f_im[..., None] * br
    eye = jnp.eye(S5_G, dtype=F32)

    def bd_in(m):
        return jnp.einsum('dgph,gk->dghkp', m, eye).reshape(2, S5_G * S5_H, S5_G * S5_P)

    def bd_out(m):
        return jnp.einsum('dghp,gk->dgpkh', m, eye).reshape(2, S5_G * S5_P, S5_G * S5_H)

    bcat = jnp.concatenate([bd_in(bb_re), bd_in(bb_im)], axis=-1).astype(BF16)
    ccat = jnp.concatenate([bd_out(c_re.astype(F32)), -bd_out(c_im.astype(F32))], axis=1).astype(BF16)
    a = jnp.concatenate([abar_re.reshape(2, 1, -1), abar_im.reshape(2, 1, -1)], axis=-1)
    return bcat, a, ccat


def _ret_kernel(qf_ref, kf_ref, vf_ref, qb_ref, kb_ref, vb_ref, dec_ref, xi_ref, zeta_ref, cd_ref,
                of_ref, ob_ref, st_scr):
    i = pl.program_id(1)
    cr = qf_ref.shape[1]

    @pl.when(i == 0)
    def _():
        st_scr[...] = jnp.zeros(st_scr.shape, F32)

    lane = lax.broadcasted_iota(jnp.int32, (cr, BRANCH_W), 1) // HEAD_W
    blk = (lax.broadcasted_iota(jnp.int32, (BRANCH_W, BRANCH_W), 0) // HEAD_W
           == lax.broadcasted_iota(jnp.int32, (BRANCH_W, BRANCH_W), 1) // HEAD_W)
    dirs = ((qf_ref, kf_ref, vf_ref, of_ref), (qb_ref, kb_ref, vb_ref, ob_ref))
    for d, (q_ref, k_ref, v_ref, o_ref) in enumerate(dirs):
        q, k, v = q_ref[0], k_ref[0], v_ref[0]
        state = st_scr[d]
        o = jnp.dot(q, state.astype(BF16), preferred_element_type=F32) * xi_ref[d]
        for h in range(N_HEADS):
            qh = jnp.where(lane == h, q, jnp.zeros_like(q))
            vh = jnp.where(lane == h, v, jnp.zeros_like(v))
            s = lax.dot_general(qh, k, (((1,), (1,)), ((), ())), preferred_element_type=F32)
            s = s * dec_ref[d, h]
            o = o + jnp.dot(s.astype(BF16), vh, preferred_element_type=F32)
        o_ref[0] = o
        kz = (k.astype(F32) * zeta_ref[d]).T.astype(BF16)
        kv = jnp.dot(kz, v, preferred_element_type=F32)
        st_scr[d] = state * cd_ref[d] + jnp.where(blk, kv, 0.0)


def _ret_call(rq, rk, rv, tabs):
    b, l, _ = rq.shape
    cr = tabs[0].shape[-1]
    n = l // cr
    fwd = pl.BlockSpec((1, cr, BRANCH_W), lambda bb, i: (bb, i, 0))
    bwd = pl.BlockSpec((1, cr, BRANCH_W), lambda bb, i: (bb, n - 1 - i, 0))
    shp = jax.ShapeDtypeStruct((b, l, BRANCH_W), F32)
    return pl.pallas_call(
        _ret_kernel,
        grid=(b, n),
        in_specs=[fwd, fwd, fwd, bwd, bwd, bwd] + [_const_spec(t.shape) for t in tabs],
        out_specs=[fwd, bwd],
        out_shape=[shp, shp],
        scratch_shapes=[pltpu.VMEM((2, BRANCH_W, BRANCH_W), F32)],
        compiler_params=_cparams(("parallel", "arbitrary")),
        name="retention",
    )(rq, rk, rv, rq, rk, rv, *tabs)


def _ret_tables(ret_log_decay, cr):
    lg = -jnp.exp(ret_log_decay.astype(F32))
    idx = jnp.arange(cr, dtype=F32)
    rel = idx[:, None] - idx[None, :]
    m_f = rel >= 0
    m_b = rel < 0
    dec_f = jnp.where(m_f[None], jnp.exp(lg[0][:, None, None] * jnp.where(m_f, rel, 0.0)[None]), 0.0)
    dec_b = jnp.where(m_b[None], jnp.exp(lg[1][:, None, None] * jnp.where(m_b, -rel, 0.0)[None]), 0.0)
    dec = jnp.stack([dec_f, dec_b])
    lane_lg = jnp.repeat(lg, HEAD_W, axis=1)
    xi = jnp.stack([jnp.exp(lane_lg[0][None, :] * (idx + 1.0)[:, None]),
                    jnp.exp(lane_lg[1][None, :] * (cr - idx)[:, None])])
    zeta = jnp.stack([jnp.exp(lane_lg[0][None, :] * (cr - 1.0 - idx)[:, None]),
                      jnp.exp(lane_lg[1][None, :] * idx[:, None])])
    cd = jnp.exp(lane_lg * cr)[:, None, :]
    return dec, xi, zeta, cd


def _merge_kernel(x_ref, g_ref, sc_ref, sh_ref, g1_ref,
                  oaT_ref, ob_ref, su_ref, yf_ref, yb_ref, rf_ref, rb_ref, rg_ref,
                  s5d_ref, gluw_ref, glub_ref, rng_ref, ones_ref,
                  wg_ref, wb_ref, wo_ref, o_ref):
    x = x_ref[0]
    h = _norm_mod(x, g_ref[...], sc_ref[0], sh_ref[0]).astype(BF16)

    o_a = oaT_ref[0].T
    o_b = ob_ref[0]

    y = s5d_ref[...] * su_ref[0] + yf_ref[0, 0] + yb_ref[0, 0]
    y = jax.nn.gelu(y)
    o_c = y * jax.nn.sigmoid(jnp.dot(y.astype(BF16), gluw_ref[...], preferred_element_type=F32)
                             + glub_ref[...])

    z = rf_ref[0] + rb_ref[0]
    z2 = z * z
    hi = z2.astype(BF16)
    lo = (z2 - hi.astype(F32)).astype(BF16)
    ss = (jnp.dot(hi, ones_ref[...], preferred_element_type=F32)
          + jnp.dot(lo, ones_ref[...], preferred_element_type=F32))
    rg = rg_ref[0]
    o_d = z * lax.rsqrt(ss * (1.0 / HEAD_W) + EPS) * rng_ref[...] * (rg * jax.nn.sigmoid(rg))

    merged = None
    for i, br in enumerate((o_a, o_b, o_c, o_d)):
        gate = jax.nn.sigmoid(jnp.dot(h, wg_ref[i], preferred_element_type=F32))
        t = gate * jnp.dot(br.astype(BF16), wb_ref[i], preferred_element_type=F32)
        merged = t if merged is None else merged + t
    out = jnp.dot(merged.astype(BF16), wo_ref[...], preferred_element_type=F32)
    o_ref[0] = x + g1_ref[0] * out


def _merge_call(geo, x, g, sc, sh, g1, oaT, ob, su, y, rf, rb, rg, mw):
    b, l, d = x.shape
    tm = geo.tm
    tok = pl.BlockSpec((1, tm, BRANCH_W), lambda bb, i: (bb, i, 0))
    xs = pl.BlockSpec((1, tm, d), lambda bb, i: (bb, i, 0))
    vec = pl.BlockSpec((1, 1, d), lambda bb, i: (bb, 0, 0))
    su_spec = pl.BlockSpec((1, tm, BRANCH_W), lambda bb, i: geo.scan_index(bb, i))

    def y_spec(dd):
        return pl.BlockSpec((1, 1, tm, BRANCH_W), lambda bb, i: (dd,) + geo.scan_index(bb, i))

    s5_d, glu_w, glu_b, ret_g, ones_bd, wg, wb, wo = mw
    return pl.pallas_call(
        _merge_kernel,
        grid=(b, geo.nt),
        in_specs=[xs, _const_spec((1, d)), vec, vec, vec,
                  pl.BlockSpec((1, BRANCH_W, tm), lambda bb, i: (bb, 0, i)), tok,
                  su_spec, y_spec(0), y_spec(1), tok, tok, tok,
                  _const_spec(s5_d.shape), _const_spec(glu_w.shape), _const_spec(glu_b.shape),
                  _const_spec(ret_g.shape), _const_spec(ones_bd.shape),
                  _const_spec(wg.shape), _const_spec(wb.shape), _const_spec(wo.shape)],
        out_specs=xs,
        out_shape=jax.ShapeDtypeStruct(x.shape, F32),
        compiler_params=_cparams(("parallel", "parallel")),
        name="merge",
    )(x, g, sc, sh, g1, oaT, ob, su, y, y, rf, rb, rg, s5_d, glu_w, glu_b, ret_g, ones_bd, wg, wb, wo)


HALO = 8


def _ffn_kernel(xp_ref, x_ref, xn_ref, g_ref, sc_ref, sh_ref, g2_ref,
                wu_ref, wgt_ref, cw_ref, cb_ref, wd_ref, fg_ref, o_ref, a_scr, *, nt, final):
    i = pl.program_id(1)
    tm = x_ref.shape[1]
    g, sc, sh = g_ref[...], sc_ref[0], sh_ref[0]
    x = x_ref[0]
    h = _norm_mod(x, g, sc, sh)
    hp = _norm_mod(xp_ref[0], g, sc, sh) * (i > 0).astype(F32)
    hn = _norm_mod(xn_ref[0], g, sc, sh) * (i < nt - 1).astype(F32)
    h_ext = jnp.concatenate([hp, h, hn], axis=0).astype(BF16)
    a_scr[...] = jnp.dot(h_ext, wu_ref[...], preferred_element_type=F32)
    cw = cw_ref[...]
    a = (a_scr[pl.ds(HALO - 1, tm), :] * cw[0:1] + a_scr[pl.ds(HALO, tm), :] * cw[1:2]
         + a_scr[pl.ds(HALO + 1, tm), :] * cw[2:3] + cb_ref[...])
    gate = jnp.dot(h.astype(BF16), wgt_ref[...], preferred_element_type=F32)
    act = (jax.nn.gelu(a) * gate).astype(BF16)
    y = x + g2_ref[0] * jnp.dot(act, wd_ref[...], preferred_element_type=F32)
    if final:
        ms = jnp.mean(y * y, axis=-1, keepdims=True)
        y = y * lax.rsqrt(ms + EPS) * fg_ref[...]
    o_ref[0] = y


def _ffn_call(geo, x, g, sc, sh, g2, fw, final_g, final):
    b, l, d = x.shape
    tm = min(geo.tm, 256)
    nt = l // tm
    r = tm // HALO
    nh = l // HALO
    xs = pl.BlockSpec((1, tm, d), lambda bb, i: (bb, i, 0))
    prev = pl.BlockSpec((1, HALO, d), lambda bb, i: (bb, jnp.maximum(i * r - 1, 0), 0))
    nxt = pl.BlockSpec((1, HALO, d), lambda bb, i: (bb, jnp.minimum((i + 1) * r, nh - 1), 0))
    vec = pl.BlockSpec((1, 1, d), lambda bb, i: (bb, 0, 0))
    wu, wgt, cw, cb, wd = fw
    once = pl.Buffered(1)

    def wspec(shape):
        return pl.BlockSpec(shape, lambda *_: (0,) * len(shape), pipeline_mode=once)

    return pl.pallas_call(
        functools.partial(_ffn_kernel, nt=nt, final=final),
        grid=(b, nt),
        in_specs=[prev, xs, nxt, _const_spec((1, d)), vec, vec, vec,
                  wspec(wu.shape), wspec(wgt.shape), _const_spec(cw.shape), _const_spec(cb.shape),
                  wspec(wd.shape), _const_spec((1, d))],
        out_specs=xs,
        out_shape=jax.ShapeDtypeStruct(x.shape, F32),
        scratch_shapes=[pltpu.VMEM((tm + 2 * HALO, D_FF), F32)],
        compiler_params=_cparams(("parallel", "parallel")),
        name="conv_ffn",
    )(x, x, x, g, sc, sh, g2, wu, wgt, cw, cb, wd, final_g)


RET_CHUNK = 256


def _layer_weights(l, p):
    d = p['w_in'].shape[1]
    ones_bd = jnp.asarray(np.kron(np.eye(N_HEADS), np.ones((HEAD_W, HEAD_W))), BF16)
    return dict(
        norm1_g=p['norm1_g'][l].reshape(1, d),
        norm2_g=p['norm2_g'][l].reshape(1, d),
        w_in=p['w_in'][l].astype(BF16),
        da_lambda=p['da_lambda'][l].astype(F32),
        da_subln_g=p['da_subln_g'][l].astype(F32),
        na_bias=_na_bias_table(p['na_rpb'][l]),
        s5=_s5_weights(p['s5_a_re'][l], p['s5_a_im'][l], p['s5_log_dt'][l], p['s5_b_re'][l],
                       p['s5_b_im'][l], p['s5_c_re'][l], p['s5_c_im'][l]),
        ret=_ret_tables(p['ret_log_decay'][l], RET_CHUNK),
        merge=(p['s5_d'][l].reshape(1, BRANCH_W).astype(F32),
               p['s5_glu_w'][l].astype(BF16),
               p['s5_glu_b'][l].reshape(1, BRANCH_W).astype(F32),
               jnp.tile(p['ret_norm_g'][l].astype(F32), N_HEADS).reshape(1, BRANCH_W),
               ones_bd,
               p['w_branch_gate'][l].astype(BF16),
               p['w_branch'][l].astype(BF16),
               p['w_out'][l].astype(BF16)),
        ffn=(p['ffn_w_up'][l].astype(BF16),
             p['ffn_w_gate'][l].astype(BF16),
             p['ffn_conv_w'][l].reshape(3, D_FF).astype(F32),
             p['ffn_conv_b'][l].reshape(1, D_FF).astype(F32),
             p['ffn_w_down'][l].astype(BF16)),
    )


def _ret_tabs_for(geo, ret_log_decay_l):
    return _ret_tables(ret_log_decay_l, min(RET_CHUNK, geo.l))


def _layer(geo, x, mod, lw, lam_init, rope, final_g, final):
    b, l, d = x.shape
    sh1, sc1, g1, sh2, sc2, g2 = [mod[:, k * d:(k + 1) * d].reshape(b, 1, d) for k in range(6)]
    (qT, k, vT, nq, nk, nv, su, rq, rk, rv, rg) = _proj_call(
        geo, x, lw['norm1_g'], sc1, sh1, lw['w_in'], rope)
    oaT = _da_call(qT, k, vT, lw['da_lambda'], lw['da_subln_g'], lam_init)
    ob = _na_call(geo, nq, nk, nv, lw['na_bias'])
    y = _s5_mixer(geo, su, lw['s5'])
    rf, rb = _ret_call(rq, rk, rv, lw['ret'])
    x = _merge_call(geo, x, lw['norm1_g'], sc1, sh1, g1, oaT, ob, su, y, rf, rb, rg, lw['merge'])
    return _ffn_call(geo, x, lw['norm2_g'], sc2, sh2, g2, lw['ffn'], final_g, final)


def kernel(x_prompt, x_sample, c_prompt, c_sample, norm1_g, norm2_g, ada_w, ada_b, w_in, da_lambda, da_subln_g, na_rpb, s5_a_re, s5_a_im, s5_log_dt, s5_b_re, s5_b_im, s5_c_re, s5_c_im, s5_d, s5_glu_w, s5_glu_b, ret_log_decay, ret_norm_g, w_branch, w_branch_gate, w_out, ffn_w_up, ffn_w_gate, ffn_conv_w, ffn_conv_b, ffn_w_down, final_norm_g):
    p = dict(norm1_g=norm1_g, norm2_g=norm2_g, w_in=w_in, da_lambda=da_lambda,
             da_subln_g=da_subln_g, na_rpb=na_rpb, s5_a_re=s5_a_re, s5_a_im=s5_a_im,
             s5_log_dt=s5_log_dt, s5_b_re=s5_b_re, s5_b_im=s5_b_im, s5_c_re=s5_c_re,
             s5_c_im=s5_c_im, s5_d=s5_d, s5_glu_w=s5_glu_w, s5_glu_b=s5_glu_b,
             ret_log_decay=ret_log_decay, ret_norm_g=ret_norm_g, w_branch=w_branch,
             w_branch_gate=w_branch_gate, w_out=w_out, ffn_w_up=ffn_w_up, ffn_w_gate=ffn_w_gate,
             ffn_conv_w=ffn_conv_w, ffn_conv_b=ffn_conv_b, ffn_w_down=ffn_w_down)
    depth = w_in.shape[0]
    d = w_in.shape[1]
    nbp = c_prompt.shape[0]
    mod = _mod_call(jnp.concatenate([c_prompt, c_sample], axis=0), ada_w, ada_b)
    lws = [_layer_weights(l, p) for l in range(depth)]
    final_g = final_norm_g.reshape(1, d).astype(F32)
    outs = []
    for x, lo in ((x_prompt, 0), (x_sample, nbp)):
        b, l, _ = x.shape
        geo = _Geom(b, l)
        rope = _rope_tables(l)
        for li in range(depth):
            lam_init = 0.8 - 0.6 * math.exp(-0.3 * li)
            x = _layer(geo, x, mod[li, lo:lo + b], lws[li], lam_init, rope, final_g,
                       final=(li == depth - 1))
        outs.append(x)
    return tuple(outs)
```

```python
import functools
import math

import numpy as np
import jax
import jax.numpy as jnp
from jax import lax
from jax.experimental import pallas as pl
from jax.experimental.pallas import tpu as pltpu

F32 = jnp.float32
BF16 = jnp.bfloat16

EPS = 1e-6
ROPE_THETA = 500000.0
BRANCH_W = 256
N_SEG = 11
DA_DIM = 32
HEAD_W = 64
N_HEADS = 4
GRID_W = 64
NA_ROWS = 8
NA_COLS = 16
S5_G = 16
S5_P = 64
S5_H = 16
S5_NSEQ = 16
D_FF = 2816
NEG = -1e30
LOG2E = 1.4426950408889634

VMEM_LIMIT = 56 * 1024 * 1024


def _cparams(sem):
    return pltpu.CompilerParams(dimension_semantics=sem, vmem_limit_bytes=VMEM_LIMIT)


def _const_spec(shape):
    nd = len(shape)
    return pl.BlockSpec(shape, lambda *_: (0,) * nd)


def _norm_mod(x, g, sc, sh):
    ms = jnp.mean(x * x, axis=-1, keepdims=True)
    y = x * lax.rsqrt(ms + EPS) * g
    return y * (1.0 + sc) + sh


def _mod_kernel(c_ref, w_ref, b_ref, o_ref):
    c = c_ref[...]
    s = c * jax.nn.sigmoid(c)
    o_ref[0] = jnp.dot(s.astype(BF16), w_ref[0].astype(BF16),
                       preferred_element_type=F32) + b_ref[0]


def _mod_call(c_all, ada_w, ada_b):
    depth, d, n6 = ada_w.shape
    nb = c_all.shape[0]
    tn = 1536
    return pl.pallas_call(
        _mod_kernel,
        grid=(depth, n6 // tn),
        in_specs=[pl.BlockSpec((nb, d), lambda l, j: (0, 0)),
                  pl.BlockSpec((1, d, tn), lambda l, j: (l, 0, j)),
                  pl.BlockSpec((1, 1, tn), lambda l, j: (l, 0, j))],
        out_specs=pl.BlockSpec((1, nb, tn), lambda l, j: (l, 0, j)),
        out_shape=jax.ShapeDtypeStruct((depth, nb, n6), F32),
        compiler_params=_cparams(("arbitrary", "arbitrary")),
        name="adaln_mod",
    )(c_all, ada_w, ada_b.reshape(depth, 1, n6))


class _Geom:
    def __init__(self, b, l):
        assert S5_NSEQ % b == 0
        self.b, self.l = b, l
        self.spb = S5_NSEQ // b
        assert l % self.spb == 0
        self.lseg = l // self.spb
        assert self.lseg & (self.lseg - 1) == 0
        self.tm = min(512, self.lseg)
        assert self.lseg % self.tm == 0
        self.tps = self.lseg // self.tm
        self.nt = l // self.tm
        self.rows = l // GRID_W
        assert self.rows >= NA_ROWS and l % (NA_RB * GRID_W) == 0

    def scan_index(self, b, i):
        n = b * self.spb + i // self.tps
        return n // 8, i % self.tps, n % 8


def _proj_kernel(x_ref, g_ref, sc_ref, sh_ref, w_ref, cos_ref, sina_ref, sinb_ref,
                 qT_ref, k_ref, vT_ref, nq_ref, nk_ref, nv_ref, su_ref,
                 rq_ref, rk_ref, rv_ref, rg_ref):
    h = _norm_mod(x_ref[0], g_ref[...], sc_ref[0], sh_ref[0]).astype(BF16)

    def seg(j):
        return jnp.dot(h, w_ref[:, j * BRANCH_W:(j + 1) * BRANCH_W],
                       preferred_element_type=F32)

    cos, sina, sinb = cos_ref[...], sina_ref[...], sinb_ref[...]

    def rot(p):
        halves = []
        for s in range(2):
            ph = p[:, s * 128:(s + 1) * 128]
            halves.append(ph * cos + pltpu.roll(ph, 124, 1) * sina + pltpu.roll(ph, 4, 1) * sinb)
        return jnp.concatenate(halves, axis=1)

    qscale = DA_DIM ** -0.5 * LOG2E
    qT_ref[0] = (rot(seg(0)) * qscale).T.astype(BF16)
    k_ref[0] = rot(seg(1)).astype(BF16)
    vT_ref[0] = seg(2).T.astype(BF16)
    nq_ref[0] = (seg(3) * HEAD_W ** -0.5).astype(BF16)
    nk_ref[0] = seg(4).astype(BF16)
    nv_ref[0] = seg(5).astype(BF16)
    su_ref[0] = seg(6)
    rq_ref[0] = seg(7).astype(BF16)
    rk_ref[0] = (seg(8) * HEAD_W ** -0.5).astype(BF16)
    rv_ref[0] = seg(9).astype(BF16)
    rg_ref[0] = seg(10)


def _proj_call(geo, x, g, sc, sh, w_in, rope):
    b, l, d = x.shape
    tm = geo.tm
    tok = pl.BlockSpec((1, tm, BRANCH_W), lambda bb, i: (bb, i, 0))
    tokT = pl.BlockSpec((1, BRANCH_W, tm), lambda bb, i: (bb, 0, i))
    vec = pl.BlockSpec((1, 1, d), lambda bb, i: (bb, 0, 0))
    tab = pl.BlockSpec((tm, 128), lambda bb, i: (i, 0))
    su_spec = pl.BlockSpec((1, tm, BRANCH_W), lambda bb, i: geo.scan_index(bb, i))
    bl = jax.ShapeDtypeStruct((b, l, BRANCH_W), BF16)
    blT = jax.ShapeDtypeStruct((b, BRANCH_W, l), BF16)
    blf = jax.ShapeDtypeStruct((b, l, BRANCH_W), F32)
    su_shape = jax.ShapeDtypeStruct((2, geo.lseg, 8 * BRANCH_W), F32)
    return pl.pallas_call(
        _proj_kernel,
        grid=(b, geo.nt),
        in_specs=[pl.BlockSpec((1, tm, d), lambda bb, i: (bb, i, 0)),
                  _const_spec((1, d)), vec, vec,
                  _const_spec((d, N_SEG * BRANCH_W)), tab, tab, tab],
        out_specs=[tokT, tok, tokT, tok, tok, tok, su_spec, tok, tok, tok, tok],
        out_shape=[blT, bl, blT, bl, bl, bl, su_shape, bl, bl, bl, blf],
        compiler_params=_cparams(("parallel", "parallel")),
        name="proj",
    )(x, g, sc, sh, w_in, *rope)


def _rope_tables(l):
    half = DA_DIM // 4 // 2
    inv_freq = jnp.power(jnp.float32(ROPE_THETA), -jnp.arange(half, dtype=F32) / half)
    ang = jnp.arange(l).astype(F32)[:, None] * inv_freq[None, :]
    cos, sin = jnp.cos(ang), jnp.sin(ang)
    zeros = jnp.zeros((l, half), F32)
    rest = DA_DIM - 2 * half

    def lanes(first, second, fill):
        comp = jnp.concatenate([first, second, jnp.full((l, rest), fill, F32)], axis=1)
        return jnp.tile(comp, (1, 128 // DA_DIM))

    return lanes(cos, cos, 1.0), lanes(-sin, zeros, 0.0), lanes(zeros, sin, 0.0)


DA_TQ = 1024
DA_TK = 1024
DA_SEED = 128
DA_SLACK = 60.0


def _da_kernel(lam_ref, g_ref, qT_ref, k_ref, vT_ref, o_ref,
               qm_scr, m_scr, l_scr, acc_scr, cur_ref, *, lam_init, nkv):
    kv = pl.program_id(2)
    n_sets = 2 * N_HEADS

    @pl.when(kv == 0)
    def _():
        qT = qT_ref[0]
        row = lax.broadcasted_iota(jnp.int32, qT.shape, 0)
        k_seed = k_ref[0, 0:DA_SEED, :]
        for hc in range(n_sets):
            keep = (row >= hc * DA_DIM) & (row < (hc + 1) * DA_DIM)
            qm = jnp.where(keep, qT, jnp.zeros_like(qT))
            qm_scr[hc] = qm
            m_scr[hc] = jnp.max(jnp.dot(k_seed, qm, preferred_element_type=F32),
                                axis=0, keepdims=True)
        l_scr[...] = jnp.zeros(l_scr.shape, F32)
        acc_scr[...] = jnp.zeros(acc_scr.shape, F32)
        cur_ref[0] = 0

    cur = cur_ref[0]
    nxt = 1 - cur
    kb = k_ref[0]

    worst = None
    for hc in range(n_sets):
        h = hc // 2
        m_ref = m_scr[hc]
        s_t = jnp.dot(kb, qm_scr[hc], preferred_element_type=F32)
        over = jnp.max(s_t, axis=0, keepdims=True) - m_ref
        worst = over if worst is None else jnp.maximum(worst, over)
        p = jnp.exp2(s_t - m_ref)
        l_scr[nxt, hc] = l_scr[cur, hc] + jnp.sum(p, axis=0, keepdims=True)
        acc_scr[nxt, hc] = acc_scr[cur, hc] + jnp.dot(
            vT_ref[0, h * HEAD_W:(h + 1) * HEAD_W, :], p.astype(BF16), preferred_element_type=F32)
    within = jnp.max(worst) <= DA_SLACK

    @pl.when(within)
    def _():
        cur_ref[0] = nxt

    @pl.when(jnp.logical_not(within))
    def _():
        for hc in range(n_sets):
            h = hc // 2
            s_t = jnp.dot(kb, qm_scr[hc], preferred_element_type=F32)
            m_old = m_scr[hc]
            m_new = jnp.maximum(m_old, jnp.max(s_t, axis=0, keepdims=True))
            p = jnp.exp2(s_t - m_new)
            alpha = jnp.exp2(m_old - m_new)
            l_scr[cur, hc] = alpha * l_scr[cur, hc] + jnp.sum(p, axis=0, keepdims=True)
            pv = jnp.dot(vT_ref[0, h * HEAD_W:(h + 1) * HEAD_W, :], p.astype(BF16),
                         preferred_element_type=F32)
            acc_scr[cur, hc] = alpha * acc_scr[cur, hc] + pv
            m_scr[hc] = m_new

    @pl.when(kv == nkv - 1)
    def _():
        fin = cur_ref[0]
        lp = lam_ref[...]
        lam = (jnp.exp(jnp.sum(lp[0:1] * lp[1:2], axis=-1, keepdims=True))
               - jnp.exp(jnp.sum(lp[2:3] * lp[3:4], axis=-1, keepdims=True)) + lam_init)
        for h in range(N_HEADS):
            o1 = acc_scr[fin, 2 * h] / l_scr[fin, 2 * h]
            o2 = acc_scr[fin, 2 * h + 1] / l_scr[fin, 2 * h + 1]
            o = o1 - lam * o2
            ms = jnp.mean(o * o, axis=0, keepdims=True)
            y = o * lax.rsqrt(ms + EPS) * g_ref[h * HEAD_W:(h + 1) * HEAD_W, :]
            o_ref[0, h * HEAD_W:(h + 1) * HEAD_W, :] = y * (1.0 - lam_init)


def _da_call(qT, k, vT, da_lambda, subln_g, lam_init):
    b, _, l = qT.shape
    tq = min(DA_TQ, l)
    tk = min(DA_TK, l)
    nkv = l // tk
    g_col = jnp.tile(subln_g, N_HEADS).reshape(BRANCH_W, 1)
    return pl.pallas_call(
        functools.partial(_da_kernel, lam_init=lam_init, nkv=nkv),
        grid=(b, l // tq, nkv),
        in_specs=[_const_spec((4, DA_DIM)), _const_spec((BRANCH_W, 1)),
                  pl.BlockSpec((1, BRANCH_W, tq), lambda bb, i, j: (bb, 0, i)),
                  pl.BlockSpec((1, tk, BRANCH_W), lambda bb, i, j: (bb, j, 0)),
                  pl.BlockSpec((1, BRANCH_W, tk), lambda bb, i, j: (bb, 0, j))],
        out_specs=pl.BlockSpec((1, BRANCH_W, tq), lambda bb, i, j: (bb, 0, i)),
        out_shape=jax.ShapeDtypeStruct((b, BRANCH_W, l), F32),
        scratch_shapes=[pltpu.VMEM((2 * N_HEADS, BRANCH_W, tq), BF16),
                        pltpu.VMEM((2 * N_HEADS, 1, tq), F32),
                        pltpu.VMEM((2, 2 * N_HEADS, 1, tq), F32),
                        pltpu.VMEM((2, 2 * N_HEADS, HEAD_W, tq), F32),
                        pltpu.SMEM((1,), jnp.int32)],
        compiler_params=_cparams(("parallel", "parallel", "arbitrary")),
        name="diff_attn",
    )(da_lambda, g_col, qT, k, vT)


NA_RB = 16
NA_WIN = NA_ROWS * GRID_W


def _na_kernel(q_ref, k_ref, v_ref, bias_ref, o_ref, *, rows):
    i = pl.program_id(1)
    lane_q = lax.broadcasted_iota(jnp.int32, (GRID_W, BRANCH_W), 1) // HEAD_W
    for j in range(NA_RB):
        r = i * NA_RB + j
        rs = jnp.clip(r - NA_ROWS // 2, 0, rows - NA_ROWS)
        start = pl.multiple_of(rs * GRID_W, GRID_W)
        kw = k_ref[0, pl.ds(start, NA_WIN), :]
        vw = v_ref[0, pl.ds(start, NA_WIN), :]
        q = q_ref[0, j * GRID_W:(j + 1) * GRID_W, :]
        qm = jnp.concatenate(
            [jnp.where(lane_q == h, q, jnp.zeros_like(q)) for h in range(N_HEADS)], axis=0)
        s = lax.dot_general(qm, kw, (((1,), (1,)), ((), ())), preferred_element_type=F32)
        s = s + bias_ref[r - rs]
        m = jnp.max(s, axis=-1, keepdims=True)
        e = jnp.exp(s - m)
        den = jnp.sum(e, axis=-1, keepdims=True)
        o_all = jnp.dot(e.astype(BF16), vw, preferred_element_type=F32) / den
        o = jnp.zeros((GRID_W, BRANCH_W), F32)
        for h in range(N_HEADS):
            o = o + jnp.where(lane_q == h, o_all[h * GRID_W:(h + 1) * GRID_W, :], 0.0)
        o_ref[0, j * GRID_W:(j + 1) * GRID_W, :] = o


def _na_call(geo, nq, nk, nv, bias):
    b, l, _ = nq.shape
    blk = NA_RB * GRID_W
    full = pl.BlockSpec((1, l, BRANCH_W), lambda bb, i: (bb, 0, 0))
    return pl.pallas_call(
        functools.partial(_na_kernel, rows=geo.rows),
        grid=(b, l // blk),
        in_specs=[pl.BlockSpec((1, blk, BRANCH_W), lambda bb, i: (bb, i, 0)), full, full,
                  _const_spec(bias.shape)],
        out_specs=pl.BlockSpec((1, blk, BRANCH_W), lambda bb, i: (bb, i, 0)),
        out_shape=jax.ShapeDtypeStruct((b, l, BRANCH_W), F32),
        compiler_params=_cparams(("parallel", "arbitrary")),
        name="nbr_attn",
    )(nq, nk, nv, bias)


def _na_bias_table(rpb):
    w = np.arange(GRID_W)
    cs = np.clip(w - NA_COLS // 2, 0, GRID_W - NA_COLS)
    kc = np.arange(GRID_W)
    allowed = (kc[None, :] >= cs[:, None]) & (kc[None, :] < cs[:, None] + NA_COLS)
    dcol = np.clip(kc[None, :] - w[:, None] + (NA_COLS - 1), 0, 2 * NA_COLS - 2)
    delta = np.arange(NA_ROWS)
    wi = np.arange(NA_ROWS)
    drow = wi[None, :] - delta[:, None] + (NA_ROWS - 1)
    sel_r = jnp.asarray(drow[:, :, None] == np.arange(2 * NA_ROWS - 1), F32)
    sel_c = jnp.asarray(dcol[:, :, None] == np.arange(2 * NA_COLS - 1), F32)
    t = jnp.einsum('xia,hab,wkb->xhwik', sel_r, rpb.astype(F32), sel_c,
                   precision=lax.Precision.HIGHEST)
    t = jnp.where(jnp.asarray(allowed)[None, None, :, None, :], t, NEG)
    return t.reshape(NA_ROWS, N_HEADS * GRID_W, NA_WIN)


S5_W2 = 2 * S5_G * S5_P


def _s5_kernel(*refs, tc, nch, emit_y):
    if emit_y:
        su_ref, b_ref, a_ref, c_ref, init_ref, y_ref, fin_ref, bu_scr, st_scr = refs
    else:
        su_ref, b_ref, a_ref, init_ref, fin_ref, bu_scr, st_scr = refs
    d = pl.program_id(0)
    c = pl.program_id(2)
    half = S5_W2 // 2

    @pl.when(c == 0)
    def _():
        st_scr[...] = init_ref[0, 0]

    rh = tc * 4
    for r in range(2):
        bu_scr[r * rh:(r + 1) * rh, :] = jnp.dot(su_ref[0, r * rh:(r + 1) * rh, :].astype(BF16),
                                                 b_ref[0], preferred_element_type=F32)
    a = a_ref[0]
    ar = jnp.broadcast_to(a[:, :half], (8, half))
    ai = jnp.broadcast_to(a[:, half:], (8, half))

    def body(t, carry):
        sr, si = carry
        tt = jnp.where(d == 0, t, tc - 1 - t)
        off = pl.multiple_of(tt * 8, 8)
        br = bu_scr[pl.ds(off, 8), 0:half]
        bi = bu_scr[pl.ds(off, 8), half:S5_W2]
        nr = ar * sr - ai * si + br
        ni = ar * si + ai * sr + bi
        if emit_y:
            bu_scr[pl.ds(off, 8), 0:half] = nr
            bu_scr[pl.ds(off, 8), half:S5_W2] = ni
        return nr, ni

    sr, si = lax.fori_loop(0, tc, body, (st_scr[:, 0:half], st_scr[:, half:S5_W2]), unroll=8)
    st_scr[:, 0:half] = sr
    st_scr[:, half:S5_W2] = si
    if emit_y:
        for r in range(2):
            y_ref[0, 0, r * rh:(r + 1) * rh, :] = jnp.dot(
                bu_scr[r * rh:(r + 1) * rh, :].astype(BF16), c_ref[0], preferred_element_type=F32)

    @pl.when(c == nch - 1)
    def _():
        fin_ref[0, 0] = st_scr[...]


def _s5_call(geo, su, bcat, a, ccat, init, emit_y):
    lseg = geo.lseg
    tc = min(256, lseg)
    nch = lseg // tc
    su2 = su.reshape(2, lseg * 8, BRANCH_W)

    def chunk(d, c):
        return jnp.where(d == 0, c, nch - 1 - c)

    in_specs = [pl.BlockSpec((1, tc * 8, BRANCH_W), lambda d, hf, c: (hf, chunk(d, c), 0)),
                pl.BlockSpec((1, BRANCH_W, S5_W2), lambda d, hf, c: (d, 0, 0)),
                pl.BlockSpec((1, 1, S5_W2), lambda d, hf, c: (d, 0, 0))]
    args = [su2, bcat, a]
    if emit_y:
        in_specs.append(pl.BlockSpec((1, S5_W2, BRANCH_W), lambda d, hf, c: (d, 0, 0)))
        args.append(ccat)
    st_spec = pl.BlockSpec((1, 1, 8, S5_W2), lambda d, hf, c: (d, hf, 0, 0))
    in_specs.append(st_spec)
    args.append(init)
    st_shape = jax.ShapeDtypeStruct((2, 2, 8, S5_W2), F32)
    if emit_y:
        out_specs = [pl.BlockSpec((1, 1, tc * 8, BRANCH_W), lambda d, hf, c: (d, hf, chunk(d, c), 0)),
                     st_spec]
        out_shape = [jax.ShapeDtypeStruct((2, 2, lseg * 8, BRANCH_W), F32), st_shape]
    else:
        out_specs = [st_spec]
        out_shape = [st_shape]
    outs = pl.pallas_call(
        functools.partial(_s5_kernel, tc=tc, nch=nch, emit_y=emit_y),
        grid=(2, 2, nch),
        in_specs=in_specs, out_specs=out_specs, out_shape=out_shape,
        scratch_shapes=[pltpu.VMEM((tc * 8, S5_W2), F32), pltpu.VMEM((8, S5_W2), F32)],
        compiler_params=_cparams(("arbitrary", "arbitrary", "arbitrary")),
        name="s5_scan" if emit_y else "s5_endstate",
    )(*args)
    return outs


def _s5_carry_kernel(fin_ref, a_ref, init_ref, *, lseg):
    half = S5_W2 // 2
    for d in range(2):
        a = a_ref[d]
        pr, pi = a[:, :half], a[:, half:]
        for _ in range(int(math.log2(lseg))):
            pr, pi = pr * pr - pi * pi, 2.0 * pr * pi
        for hf in range(2):
            sr = jnp.zeros((1, half), F32)
            si = jnp.zeros((1, half), F32)
            order = range(8) if d == 0 else range(7, -1, -1)
            for s in order:
                init_ref[d, hf, s:s + 1, 0:half] = sr
                init_ref[d, hf, s:s + 1, half:S5_W2] = si
                fr = fin_ref[d, hf, s:s + 1, 0:half]
                fi = fin_ref[d, hf, s:s + 1, half:S5_W2]
                sr, si = pr * sr - pi * si + fr, pr * si + pi * sr + fi


def _s5_carry_call(geo, fin, a):
    return pl.pallas_call(
        functools.partial(_s5_carry_kernel, lseg=geo.lseg),
        out_shape=jax.ShapeDtypeStruct(fin.shape, F32),
        name="s5_carry",
    )(fin, a)


def _s5_mixer(geo, su, s5w):
    bcat, a, ccat = s5w
    zeros = jnp.zeros((2, 2, 8, S5_W2), F32)
    if geo.spb == 1:
        init = zeros
    else:
        assert geo.spb == 8
        (fin,) = _s5_call(geo, su, bcat, a, None, zeros, emit_y=False)
        init = _s5_carry_call(geo, fin, a)
    y, _ = _s5_call(geo, su, bcat, a, ccat, init, emit_y=True)
    return y.reshape(2, 2, geo.lseg, 8 * BRANCH_W)


def _s5_weights(a_re, a_im, log_dt, b_re, b_im, c_re, c_im):
    lam_re = jnp.minimum(a_re.astype(F32), -1e-4)
    lam_im = a_im.astype(F32)
    dt = jnp.exp(log_dt.astype(F32))[..., None]
    mag = jnp.exp(lam_re * dt)
    abar_re = mag * jnp.cos(lam_im * dt)
    abar_im = mag * jnp.sin(lam_im * dt)
    den = lam_re * lam_re + lam_im * lam_im
    f_re = ((abar_re - 1.0) * lam_re + abar_im * lam_im) / den
    f_im = (abar_im * lam_re - (abar_re - 1.0) * lam_im) / den
    br, bi = b_re.astype(F32), b_im.astype(F32)
    bb_re = f_re[..., None] * br - f_im[..., None] * bi
    bb_im = f_re[..., None] * bi + f_im[..., None] * br
    eye = jnp.eye(S5_G, dtype=F32)

    def bd_in(m):
        return jnp.einsum('dgph,gk->dghkp', m, eye).reshape(2, S5_G * S5_H, S5_G * S5_P)

    def bd_out(m):
        return jnp.einsum('dghp,gk->dgpkh', m, eye).reshape(2, S5_G * S5_P, S5_G * S5_H)

    bcat = jnp.concatenate([bd_in(bb_re), bd_in(bb_im)], axis=-1).astype(BF16)
    ccat = jnp.concatenate([bd_out(c_re.astype(F32)), -bd_out(c_im.astype(F32))], axis=1).astype(BF16)
    a = jnp.concatenate([abar_re.reshape(2, 1, -1), abar_im.reshape(2, 1, -1)], axis=-1)
    return bcat, a, ccat


def _ret_kernel(qf_ref, kf_ref, vf_ref, qb_ref, kb_ref, vb_ref, dec_ref, xi_ref, zeta_ref, cd_ref,
                of_ref, ob_ref, st_scr):
    i = pl.program_id(1)
    cr = qf_ref.shape[1]

    @pl.when(i == 0)
    def _():
        st_scr[...] = jnp.zeros(st_scr.shape, F32)

    lane = lax.broadcasted_iota(jnp.int32, (cr, BRANCH_W), 1) // HEAD_W
    blk = (lax.broadcasted_iota(jnp.int32, (BRANCH_W, BRANCH_W), 0) // HEAD_W
           == lax.broadcasted_iota(jnp.int32, (BRANCH_W, BRANCH_W), 1) // HEAD_W)
    dirs = ((qf_ref, kf_ref, vf_ref, of_ref), (qb_ref, kb_ref, vb_ref, ob_ref))
    for d, (q_ref, k_ref, v_ref, o_ref) in enumerate(dirs):
        q, k, v = q_ref[0], k_ref[0], v_ref[0]
        state = st_scr[d]
        o = jnp.dot(q, state.astype(BF16), preferred_element_type=F32) * xi_ref[d]
        for h in range(N_HEADS):
            qh = jnp.where(lane == h, q, jnp.zeros_like(q))
            vh = jnp.where(lane == h, v, jnp.zeros_like(v))
            s = lax.dot_general(qh, k, (((1,), (1,)), ((), ())), preferred_element_type=F32)
            s = s * dec_ref[d, h]
            o = o + jnp.dot(s.astype(BF16), vh, preferred_element_type=F32)
        o_ref[0] = o
        kz = (k.astype(F32) * zeta_ref[d]).T.astype(BF16)
        kv = jnp.dot(kz, v, preferred_element_type=F32)
        st_scr[d] = state * cd_ref[d] + jnp.where(blk, kv, 0.0)


def _ret_call(rq, rk, rv, tabs):
    b, l, _ = rq.shape
    cr = tabs[0].shape[-1]
    n = l // cr
    fwd = pl.BlockSpec((1, cr, BRANCH_W), lambda bb, i: (bb, i, 0))
    bwd = pl.BlockSpec((1, cr, BRANCH_W), lambda bb, i: (bb, n - 1 - i, 0))
    shp = jax.ShapeDtypeStruct((b, l, BRANCH_W), F32)
    return pl.pallas_call(
        _ret_kernel,
        grid=(b, n),
        in_specs=[fwd, fwd, fwd, bwd, bwd, bwd] + [_const_spec(t.shape) for t in tabs],
        out_specs=[fwd, bwd],
        out_shape=[shp, shp],
        scratch_shapes=[pltpu.VMEM((2, BRANCH_W, BRANCH_W), F32)],
        compiler_params=_cparams(("parallel", "arbitrary")),
        name="retention",
    )(rq, rk, rv, rq, rk, rv, *tabs)


def _ret_tables(ret_log_decay, cr):
    lg = -jnp.exp(ret_log_decay.astype(F32))
    idx = jnp.arange(cr, dtype=F32)
    rel = idx[:, None] - idx[None, :]
    m_f = rel >= 0
    m_b = rel < 0
    dec_f = jnp.where(m_f[None], jnp.exp(lg[0][:, None, None] * jnp.where(m_f, rel, 0.0)[None]), 0.0)
    dec_b = jnp.where(m_b[None], jnp.exp(lg[1][:, None, None] * jnp.where(m_b, -rel, 0.0)[None]), 0.0)
    dec = jnp.stack([dec_f, dec_b])
    lane_lg = jnp.repeat(lg, HEAD_W, axis=1)
    xi = jnp.stack([jnp.exp(lane_lg[0][None, :] * (idx + 1.0)[:, None]),
                    jnp.exp(lane_lg[1][None, :] * (cr - idx)[:, None])])
    zeta = jnp.stack([jnp.exp(lane_lg[0][None, :] * (cr - 1.0 - idx)[:, None]),
                      jnp.exp(lane_lg[1][None, :] * idx[:, None])])
    cd = jnp.exp(lane_lg * cr)[:, None, :]
    return dec, xi, zeta, cd


def _merge_kernel(x_ref, g_ref, sc_ref, sh_ref, g1_ref,
                  oaT_ref, ob_ref, su_ref, yf_ref, yb_ref, rf_ref, rb_ref, rg_ref,
                  s5d_ref, gluw_ref, glub_ref, rng_ref, ones_ref,
                  wg_ref, wb_ref, wo_ref, o_ref):
    tm = x_ref.shape[1]
    nh = 2 if tm % 256 == 0 else 1
    rh = tm // nh
    o_a_full = oaT_ref[0].T
    for hh in range(nh):
        rs = slice(hh * rh, (hh + 1) * rh)
        x = x_ref[0, rs, :]
        h = _norm_mod(x, g_ref[...], sc_ref[0], sh_ref[0]).astype(BF16)

        o_a = o_a_full[rs, :]
        o_b = ob_ref[0, rs, :]

        y = s5d_ref[...] * su_ref[0, rs, :] + yf_ref[0, 0, rs, :] + yb_ref[0, 0, rs, :]
        y = jax.nn.gelu(y)
        o_c = y * jax.nn.sigmoid(jnp.dot(y.astype(BF16), gluw_ref[...], preferred_element_type=F32)
                                 + glub_ref[...])

        z = rf_ref[0, rs, :] + rb_ref[0, rs, :]
        z2 = z * z
        hi = z2.astype(BF16)
        lo = (z2 - hi.astype(F32)).astype(BF16)
        ss = (jnp.dot(hi, ones_ref[...], preferred_element_type=F32)
              + jnp.dot(lo, ones_ref[...], preferred_element_type=F32))
        rg = rg_ref[0, rs, :]
        o_d = z * lax.rsqrt(ss * (1.0 / HEAD_W) + EPS) * rng_ref[...] * (rg * jax.nn.sigmoid(rg))

        merged = None
        for i, br in enumerate((o_a, o_b, o_c, o_d)):
            gate = jax.nn.sigmoid(jnp.dot(h, wg_ref[i], preferred_element_type=F32))
            t = gate * jnp.dot(br.astype(BF16), wb_ref[i], preferred_element_type=F32)
            merged = t if merged is None else merged + t
        out = jnp.dot(merged.astype(BF16), wo_ref[...], preferred_element_type=F32)
        o_ref[0, rs, :] = x + g1_ref[0] * out


def _merge_call(geo, x, g, sc, sh, g1, oaT, ob, su, y, rf, rb, rg, mw):
    b, l, d = x.shape
    tm = geo.tm
    tok = pl.BlockSpec((1, tm, BRANCH_W), lambda bb, i: (bb, i, 0))
    xs = pl.BlockSpec((1, tm, d), lambda bb, i: (bb, i, 0))
    vec = pl.BlockSpec((1, 1, d), lambda bb, i: (bb, 0, 0))
    su_spec = pl.BlockSpec((1, tm, BRANCH_W), lambda bb, i: geo.scan_index(bb, i))

    def y_spec(dd):
        return pl.BlockSpec((1, 1, tm, BRANCH_W), lambda bb, i: (dd,) + geo.scan_index(bb, i))

    s5_d, glu_w, glu_b, ret_g, ones_bd, wg, wb, wo = mw
    return pl.pallas_call(
        _merge_kernel,
        grid=(b, geo.nt),
        in_specs=[xs, _const_spec((1, d)), vec, vec, vec,
                  pl.BlockSpec((1, BRANCH_W, tm), lambda bb, i: (bb, 0, i)), tok,
                  su_spec, y_spec(0), y_spec(1), tok, tok, tok,
                  _const_spec(s5_d.shape), _const_spec(glu_w.shape), _const_spec(glu_b.shape),
                  _const_spec(ret_g.shape), _const_spec(ones_bd.shape),
                  _const_spec(wg.shape), _const_spec(wb.shape), _const_spec(wo.shape)],
        out_specs=xs,
        out_shape=jax.ShapeDtypeStruct(x.shape, F32),
        compiler_params=_cparams(("parallel", "parallel")),
        name="merge",
    )(x, g, sc, sh, g1, oaT, ob, su, y, y, rf, rb, rg, s5_d, glu_w, glu_b, ret_g, ones_bd, wg, wb, wo)


HALO = 8


def _ffn_kernel(xp_ref, x_ref, xn_ref, g_ref, sc_ref, sh_ref, g2_ref,
                wu_ref, wgt_ref, cw_ref, cb_ref, wd_ref, fg_ref, o_ref, a_scr, *, nt, final):
    i = pl.program_id(1)
    tm = x_ref.shape[1]
    g, sc, sh = g_ref[...], sc_ref[0], sh_ref[0]
    x = x_ref[0]
    h = _norm_mod(x, g, sc, sh)
    hp = _norm_mod(xp_ref[0], g, sc, sh) * (i > 0).astype(F32)
    hn = _norm_mod(xn_ref[0], g, sc, sh) * (i < nt - 1).astype(F32)
    h_ext = jnp.concatenate([hp, h, hn], axis=0).astype(BF16)
    a_scr[...] = jnp.dot(h_ext, wu_ref[...], preferred_element_type=F32)
    cw = cw_ref[...]
    a = (a_scr[pl.ds(HALO - 1, tm), :] * cw[0:1] + a_scr[pl.ds(HALO, tm), :] * cw[1:2]
         + a_scr[pl.ds(HALO + 1, tm), :] * cw[2:3] + cb_ref[...])
    gate = jnp.dot(h.astype(BF16), wgt_ref[...], preferred_element_type=F32)
    act = (jax.nn.gelu(a) * gate).astype(BF16)
    y = x + g2_ref[0] * jnp.dot(act, wd_ref[...], preferred_element_type=F32)
    if final:
        ms = jnp.mean(y * y, axis=-1, keepdims=True)
        y = y * lax.rsqrt(ms + EPS) * fg_ref[...]
    o_ref[0] = y


def _ffn_call(geo, x, g, sc, sh, g2, fw, final_g, final):
    b, l, d = x.shape
    tm = min(geo.tm, 512)
    nt = l // tm
    r = tm // HALO
    nh = l // HALO
    xs = pl.BlockSpec((1, tm, d), lambda bb, i: (bb, i, 0))
    prev = pl.BlockSpec((1, HALO, d), lambda bb, i: (bb, jnp.maximum(i * r - 1, 0), 0))
    nxt = pl.BlockSpec((1, HALO, d), lambda bb, i: (bb, jnp.minimum((i + 1) * r, nh - 1), 0))
    vec = pl.BlockSpec((1, 1, d), lambda bb, i: (bb, 0, 0))
    wu, wgt, cw, cb, wd = fw
    once = pl.Buffered(1)

    def wspec(shape):
        return pl.BlockSpec(shape, lambda *_: (0,) * len(shape), pipeline_mode=once)

    return pl.pallas_call(
        functools.partial(_ffn_kernel, nt=nt, final=final),
        grid=(b, nt),
        in_specs=[prev, xs, nxt, _const_spec((1, d)), vec, vec, vec,
                  wspec(wu.shape), wspec(wgt.shape), _const_spec(cw.shape), _const_spec(cb.shape),
                  wspec(wd.shape), _const_spec((1, d))],
        out_specs=xs,
        out_shape=jax.ShapeDtypeStruct(x.shape, F32),
        scratch_shapes=[pltpu.VMEM((tm + 2 * HALO, D_FF), F32)],
        compiler_params=_cparams(("parallel", "parallel")),
        name="conv_ffn",
    )(x, x, x, g, sc, sh, g2, wu, wgt, cw, cb, wd, final_g)


RET_CHUNK = 256


def _layer_weights(l, p):
    d = p['w_in'].shape[1]
    ones_bd = jnp.asarray(np.kron(np.eye(N_HEADS), np.ones((HEAD_W, HEAD_W))), BF16)
    return dict(
        norm1_g=p['norm1_g'][l].reshape(1, d),
        norm2_g=p['norm2_g'][l].reshape(1, d),
        w_in=p['w_in'][l].astype(BF16),
        da_lambda=p['da_lambda'][l].astype(F32),
        da_subln_g=p['da_subln_g'][l].astype(F32),
        na_bias=_na_bias_table(p['na_rpb'][l]),
        s5=_s5_weights(p['s5_a_re'][l], p['s5_a_im'][l], p['s5_log_dt'][l], p['s5_b_re'][l],
                       p['s5_b_im'][l], p['s5_c_re'][l], p['s5_c_im'][l]),
        ret=_ret_tables(p['ret_log_decay'][l], RET_CHUNK),
        merge=(p['s5_d'][l].reshape(1, BRANCH_W).astype(F32),
               p['s5_glu_w'][l].astype(BF16),
               p['s5_glu_b'][l].reshape(1, BRANCH_W).astype(F32),
               jnp.tile(p['ret_norm_g'][l].astype(F32), N_HEADS).reshape(1, BRANCH_W),
               ones_bd,
               p['w_branch_gate'][l].astype(BF16),
               p['w_branch'][l].astype(BF16),
               p['w_out'][l].astype(BF16)),
        ffn=(p['ffn_w_up'][l].astype(BF16),
             p['ffn_w_gate'][l].astype(BF16),
             p['ffn_conv_w'][l].reshape(3, D_FF).astype(F32),
             p['ffn_conv_b'][l].reshape(1, D_FF).astype(F32),
             p['ffn_w_down'][l].astype(BF16)),
    )


def _ret_tabs_for(geo, ret_log_decay_l):
    return _ret_tables(ret_log_decay_l, min(RET_CHUNK, geo.l))


def _layer(geo, x, mod, lw, lam_init, rope, final_g, final):
    b, l, d = x.shape
    sh1, sc1, g1, sh2, sc2, g2 = [mod[:, k * d:(k + 1) * d].reshape(b, 1, d) for k in range(6)]
    (qT, k, vT, nq, nk, nv, su, rq, rk, rv, rg) = _proj_call(
        geo, x, lw['norm1_g'], sc1, sh1, lw['w_in'], rope)
    oaT = _da_call(qT, k, vT, lw['da_lambda'], lw['da_subln_g'], lam_init)
    ob = _na_call(geo, nq, nk, nv, lw['na_bias'])
    y = _s5_mixer(geo, su, lw['s5'])
    rf, rb = _ret_call(rq, rk, rv, lw['ret'])
    x = _merge_call(geo, x, lw['norm1_g'], sc1, sh1, g1, oaT, ob, su, y, rf, rb, rg, lw['merge'])
    return _ffn_call(geo, x, lw['norm2_g'], sc2, sh2, g2, lw['ffn'], final_g, final)


def kernel(x_prompt, x_sample, c_prompt, c_sample, norm1_g, norm2_g, ada_w, ada_b, w_in, da_lambda, da_subln_g, na_rpb, s5_a_re, s5_a_im, s5_log_dt, s5_b_re, s5_b_im, s5_c_re, s5_c_im, s5_d, s5_glu_w, s5_glu_b, ret_log_decay, ret_norm_g, w_branch, w_branch_gate, w_out, ffn_w_up, ffn_w_gate, ffn_conv_w, ffn_conv_b, ffn_w_down, final_norm_g):
    p = dict(norm1_g=norm1_g, norm2_g=norm2_g, w_in=w_in, da_lambda=da_lambda,
             da_subln_g=da_subln_g, na_rpb=na_rpb, s5_a_re=s5_a_re, s5_a_im=s5_a_im,
             s5_log_dt=s5_log_dt, s5_b_re=s5_b_re, s5_b_im=s5_b_im, s5_c_re=s5_c_re,
             s5_c_im=s5_c_im, s5_d=s5_d, s5_glu_w=s5_glu_w, s5_glu_b=s5_glu_b,
             ret_log_decay=ret_log_decay, ret_norm_g=ret_norm_g, w_branch=w_branch,
             w_branch_gate=w_branch_gate, w_out=w_out, ffn_w_up=ffn_w_up, ffn_w_gate=ffn_w_gate,
             ffn_conv_w=ffn_conv_w, ffn_conv_b=ffn_conv_b, ffn_w_down=ffn_w_down)
    depth = w_in.shape[0]
    d = w_in.shape[1]
    nbp = c_prompt.shape[0]
    mod = _mod_call(jnp.concatenate([c_prompt, c_sample], axis=0), ada_w, ada_b)
    lws = [_layer_weights(l, p) for l in range(depth)]
    final_g = final_norm_g.reshape(1, d).astype(F32)
    outs = []
    for x, lo in ((x_prompt, 0), (x_sample, nbp)):
        b, l, _ = x.shape
        geo = _Geom(b, l)
        rope = _rope_tables(l)
        for li in range(depth):
            lam_init = 0.8 - 0.6 * math.exp(-0.3 * li)
            x = _layer(geo, x, mod[li, lo:lo + b], lws[li], lam_init, rope, final_g,
                       final=(li == depth - 1))
        outs.append(x)
    return tuple(outs)
```

```python
import functools
import math

import numpy as np
import jax
import jax.numpy as jnp
from jax import lax
from jax.experimental import pallas as pl
from jax.experimental.pallas import tpu as pltpu

F32 = jnp.float32
BF16 = jnp.bfloat16

EPS = 1e-6
ROPE_THETA = 500000.0
BRANCH_W = 256
N_SEG = 11
DA_DIM = 32
HEAD_W = 64
N_HEADS = 4
GRID_W = 64
NA_ROWS = 8
NA_COLS = 16
S5_G = 16
S5_P = 64
S5_H = 16
S5_NSEQ = 16
D_FF = 2816
NEG = -1e30
LOG2E = 1.4426950408889634

VMEM_LIMIT = 56 * 1024 * 1024


def _cparams(sem):
    return pltpu.CompilerParams(dimension_semantics=sem, vmem_limit_bytes=VMEM_LIMIT)


def _const_spec(shape):
    nd = len(shape)
    return pl.BlockSpec(shape, lambda *_: (0,) * nd)


def _norm_mod(x, g, sc, sh):
    ms = jnp.mean(x * x, axis=-1, keepdims=True)
    y = x * lax.rsqrt(ms + EPS) * g
    return y * (1.0 + sc) + sh


def _mod_kernel(c_ref, w_ref, b_ref, o_ref):
    c = c_ref[...]
    s = c * jax.nn.sigmoid(c)
    o_ref[0] = jnp.dot(s.astype(BF16), w_ref[0].astype(BF16),
                       preferred_element_type=F32) + b_ref[0]


def _mod_call(c_all, ada_w, ada_b):
    depth, d, n6 = ada_w.shape
    nb = c_all.shape[0]
    tn = 1536
    return pl.pallas_call(
        _mod_kernel,
        grid=(depth, n6 // tn),
        in_specs=[pl.BlockSpec((nb, d), lambda l, j: (0, 0)),
                  pl.BlockSpec((1, d, tn), lambda l, j: (l, 0, j)),
                  pl.BlockSpec((1, 1, tn), lambda l, j: (l, 0, j))],
        out_specs=pl.BlockSpec((1, nb, tn), lambda l, j: (l, 0, j)),
        out_shape=jax.ShapeDtypeStruct((depth, nb, n6), F32),
        compiler_params=_cparams(("arbitrary", "arbitrary")),
        name="adaln_mod",
    )(c_all, ada_w, ada_b.reshape(depth, 1, n6))


class _Geom:
    def __init__(self, b, l):
        assert S5_NSEQ % b == 0
        self.b, self.l = b, l
        self.spb = S5_NSEQ // b
        assert l % self.spb == 0
        self.lseg = l // self.spb
        assert self.lseg & (self.lseg - 1) == 0
        self.tm = min(512, self.lseg)
        assert self.lseg % self.tm == 0
        self.tps = self.lseg // self.tm
        self.nt = l // self.tm
        self.rows = l // GRID_W
        assert self.rows >= NA_ROWS and l % (NA_RB * GRID_W) == 0

    def scan_index(self, b, i):
        n = b * self.spb + i // self.tps
        return n // 8, i % self.tps, n % 8


def _proj_kernel(x_ref, g_ref, sc_ref, sh_ref, w_ref, cos_ref, sina_ref, sinb_ref, gsum_ref,
                 qT_ref, k_ref, kn_ref, vT_ref, nq_ref, nk_ref, nv_ref, su_ref,
                 rq_ref, rk_ref, rv_ref, rg_ref):
    h = _norm_mod(x_ref[0], g_ref[...], sc_ref[0], sh_ref[0]).astype(BF16)

    def seg(j):
        return jnp.dot(h, w_ref[:, j * BRANCH_W:(j + 1) * BRANCH_W],
                       preferred_element_type=F32)

    cos, sina, sinb = cos_ref[...], sina_ref[...], sinb_ref[...]

    def rot(p):
        halves = []
        for s in range(2):
            ph = p[:, s * 128:(s + 1) * 128]
            halves.append(ph * cos + pltpu.roll(ph, 124, 1) * sina + pltpu.roll(ph, 4, 1) * sinb)
        return jnp.concatenate(halves, axis=1)

    qscale = DA_DIM ** -0.5 * LOG2E
    qT_ref[0] = (rot(seg(0)) * qscale).T.astype(BF16)
    k_bf = rot(seg(1)).astype(BF16)
    k_ref[0] = k_bf
    k_f = k_bf.astype(F32)
    k_sq = jnp.dot((k_f * k_f).astype(BF16), gsum_ref[...], preferred_element_type=F32)
    kn_ref[0, 0] = jnp.max(k_sq, axis=0, keepdims=True)
    vT_ref[0] = seg(2).T.astype(BF16)
    nq_ref[0] = (seg(3) * HEAD_W ** -0.5).astype(BF16)
    nk_ref[0] = seg(4).astype(BF16)
    nv_ref[0] = seg(5).astype(BF16)
    su_ref[0] = seg(6)
    rq_ref[0] = seg(7).astype(BF16)
    rk_ref[0] = (seg(8) * HEAD_W ** -0.5).astype(BF16)
    rv_ref[0] = seg(9).astype(BF16)
    rg_ref[0] = seg(10)


def _proj_call(geo, x, g, sc, sh, w_in, rope, gsum):
    b, l, d = x.shape
    tm = geo.tm
    tok = pl.BlockSpec((1, tm, BRANCH_W), lambda bb, i: (bb, i, 0))
    tokT = pl.BlockSpec((1, BRANCH_W, tm), lambda bb, i: (bb, 0, i))
    vec = pl.BlockSpec((1, 1, d), lambda bb, i: (bb, 0, 0))
    tab = pl.BlockSpec((tm, 128), lambda bb, i: (i, 0))
    su_spec = pl.BlockSpec((1, tm, BRANCH_W), lambda bb, i: geo.scan_index(bb, i))
    bl = jax.ShapeDtypeStruct((b, l, BRANCH_W), BF16)
    blT = jax.ShapeDtypeStruct((b, BRANCH_W, l), BF16)
    blf = jax.ShapeDtypeStruct((b, l, BRANCH_W), F32)
    su_shape = jax.ShapeDtypeStruct((2, geo.lseg, 8 * BRANCH_W), F32)
    kn_spec = pl.BlockSpec((1, 1, 1, BRANCH_W), lambda bb, i: (bb, i, 0, 0))
    kn_shape = jax.ShapeDtypeStruct((b, geo.nt, 1, BRANCH_W), F32)
    return pl.pallas_call(
        _proj_kernel,
        grid=(b, geo.nt),
        in_specs=[pl.BlockSpec((1, tm, d), lambda bb, i: (bb, i, 0)),
                  _const_spec((1, d)), vec, vec,
                  _const_spec((d, N_SEG * BRANCH_W)), tab, tab, tab, _const_spec(gsum.shape)],
        out_specs=[tokT, tok, kn_spec, tokT, tok, tok, tok, su_spec, tok, tok, tok, tok],
        out_shape=[blT, bl, kn_shape, blT, bl, bl, bl, su_shape, bl, bl, bl, blf],
        compiler_params=_cparams(("parallel", "parallel")),
        name="proj",
    )(x, g, sc, sh, w_in, *rope, gsum)


def _rope_tables(l):
    half = DA_DIM // 4 // 2
    inv_freq = jnp.power(jnp.float32(ROPE_THETA), -jnp.arange(half, dtype=F32) / half)
    ang = jnp.arange(l).astype(F32)[:, None] * inv_freq[None, :]
    cos, sin = jnp.cos(ang), jnp.sin(ang)
    zeros = jnp.zeros((l, half), F32)
    rest = DA_DIM - 2 * half

    def lanes(first, second, fill):
        comp = jnp.concatenate([first, second, jnp.full((l, rest), fill, F32)], axis=1)
        return jnp.tile(comp, (1, 128 // DA_DIM))

    return lanes(cos, cos, 1.0), lanes(-sin, zeros, 0.0), lanes(zeros, sin, 0.0)


DA_TQ = 1024
DA_TK = 1024
DA_SEED = 128
DA_SLACK = 60.0
DA_BOUND_PAD = 1.02


def _da_kernel(lam_ref, g_ref, kn_ref, qT_ref, k_ref, vT_ref, o_ref,
               qm_scr, m_scr, l_scr, acc_scr, mode_ref, *, lam_init, nkv):
    kv = pl.program_id(2)
    n_sets = 2 * N_HEADS

    @pl.when(kv == 0)
    def _():
        qT = qT_ref[0]
        row = lax.broadcasted_iota(jnp.int32, qT.shape, 0)
        k_seed = k_ref[0, 0:DA_SEED, :]
        qf = qT.astype(F32)
        q2 = qf * qf
        k2max = jnp.max(jnp.max(kn_ref[0], axis=0), axis=-1, keepdims=True)
        k_bound = jnp.sqrt(k2max * DA_BOUND_PAD)
        worst = None
        for hc in range(n_sets):
            keep = (row >= hc * DA_DIM) & (row < (hc + 1) * DA_DIM)
            qm = jnp.where(keep, qT, jnp.zeros_like(qT))
            qm_scr[hc] = qm
            m0 = jnp.max(jnp.dot(k_seed, qm, preferred_element_type=F32), axis=0, keepdims=True)
            m_scr[hc] = m0
            q_norm = jnp.sqrt(jnp.sum(q2[hc * DA_DIM:(hc + 1) * DA_DIM, :], axis=0, keepdims=True))
            over = q_norm * k_bound - m0
            worst = over if worst is None else jnp.maximum(worst, over)
        mode_ref[0] = (jnp.max(worst) <= DA_SLACK).astype(jnp.int32)
        l_scr[...] = jnp.zeros(l_scr.shape, F32)
        acc_scr[...] = jnp.zeros(acc_scr.shape, F32)

    kb = k_ref[0]
    fixed_shift = mode_ref[0] == 1

    @pl.when(fixed_shift)
    def _():
        for hc in range(n_sets):
            h = hc // 2
            s_t = jnp.dot(kb, qm_scr[hc], preferred_element_type=F32)
            p = jnp.exp2(s_t - m_scr[hc])
            l_scr[hc] = l_scr[hc] + jnp.sum(p, axis=0, keepdims=True)
            acc_scr[hc] = acc_scr[hc] + jnp.dot(
                vT_ref[0, h * HEAD_W:(h + 1) * HEAD_W, :], p.astype(BF16), preferred_element_type=F32)

    @pl.when(jnp.logical_not(fixed_shift))
    def _():
        for hc in range(n_sets):
            h = hc // 2
            s_t = jnp.dot(kb, qm_scr[hc], preferred_element_type=F32)
            m_old = m_scr[hc]
            m_new = jnp.maximum(m_old, jnp.max(s_t, axis=0, keepdims=True))
            p = jnp.exp2(s_t - m_new)
            alpha = jnp.exp2(m_old - m_new)
            l_scr[hc] = alpha * l_scr[hc] + jnp.sum(p, axis=0, keepdims=True)
            pv = jnp.dot(vT_ref[0, h * HEAD_W:(h + 1) * HEAD_W, :], p.astype(BF16),
                         preferred_element_type=F32)
            acc_scr[hc] = alpha * acc_scr[hc] + pv
            m_scr[hc] = m_new

    @pl.when(kv == nkv - 1)
    def _():
        lp = lam_ref[...]
        lam = (jnp.exp(jnp.sum(lp[0:1] * lp[1:2], axis=-1, keepdims=True))
               - jnp.exp(jnp.sum(lp[2:3] * lp[3:4], axis=-1, keepdims=True)) + lam_init)
        for h in range(N_HEADS):
            o1 = acc_scr[2 * h] / l_scr[2 * h]
            o2 = acc_scr[2 * h + 1] / l_scr[2 * h + 1]
            o = o1 - lam * o2
            ms = jnp.mean(o * o, axis=0, keepdims=True)
            y = o * lax.rsqrt(ms + EPS) * g_ref[h * HEAD_W:(h + 1) * HEAD_W, :]
            o_ref[0, h * HEAD_W:(h + 1) * HEAD_W, :] = y * (1.0 - lam_init)


def _da_call(qT, k, vT, kn, da_lambda, subln_g, lam_init):
    b, _, l = qT.shape
    tq = min(DA_TQ, l)
    tk = min(DA_TK, l)
    nkv = l // tk
    g_col = jnp.tile(subln_g, N_HEADS).reshape(BRANCH_W, 1)
    return pl.pallas_call(
        functools.partial(_da_kernel, lam_init=lam_init, nkv=nkv),
        grid=(b, l // tq, nkv),
        in_specs=[_const_spec((4, DA_DIM)), _const_spec((BRANCH_W, 1)),
                  pl.BlockSpec((1,) + kn.shape[1:], lambda bb, i, j: (bb, 0, 0, 0)),
                  pl.BlockSpec((1, BRANCH_W, tq), lambda bb, i, j: (bb, 0, i)),
                  pl.BlockSpec((1, tk, BRANCH_W), lambda bb, i, j: (bb, j, 0)),
                  pl.BlockSpec((1, BRANCH_W, tk), lambda bb, i, j: (bb, 0, j))],
        out_specs=pl.BlockSpec((1, BRANCH_W, tq), lambda bb, i, j: (bb, 0, i)),
        out_shape=jax.ShapeDtypeStruct((b, BRANCH_W, l), F32),
        scratch_shapes=[pltpu.VMEM((2 * N_HEADS, BRANCH_W, tq), BF16),
                        pltpu.VMEM((2 * N_HEADS, 1, tq), F32),
                        pltpu.VMEM((2 * N_HEADS, 1, tq), F32),
                        pltpu.VMEM((2 * N_HEADS, HEAD_W, tq), F32),
                        pltpu.SMEM((1,), jnp.int32)],
        compiler_params=_cparams(("parallel", "parallel", "arbitrary")),
        name="diff_attn",
    )(da_lambda, g_col, kn, qT, k, vT)


NA_RB = 16
NA_WIN = NA_ROWS * GRID_W


def _na_kernel(q_ref, k_ref, v_ref, bias_ref, o_ref, *, rows):
    i = pl.program_id(1)
    lane_q = lax.broadcasted_iota(jnp.int32, (GRID_W, BRANCH_W), 1) // HEAD_W
    for j in range(NA_RB):
        r = i * NA_RB + j
        rs = jnp.clip(r - NA_ROWS // 2, 0, rows - NA_ROWS)
        start = pl.multiple_of(rs * GRID_W, GRID_W)
        kw = k_ref[0, pl.ds(start, NA_WIN), :]
        vw = v_ref[0, pl.ds(start, NA_WIN), :]
        q = q_ref[0, j * GRID_W:(j + 1) * GRID_W, :]
        qm = jnp.concatenate(
            [jnp.where(lane_q == h, q, jnp.zeros_like(q)) for h in range(N_HEADS)], axis=0)
        s = lax.dot_general(qm, kw, (((1,), (1,)), ((), ())), preferred_element_type=F32)
        s = s + bias_ref[r - rs]
        m = jnp.max(s, axis=-1, keepdims=True)
        e = jnp.exp(s - m)
        den = jnp.sum(e, axis=-1, keepdims=True)
        o_all = jnp.dot(e.astype(BF16), vw, preferred_element_type=F32) / den
        o = jnp.zeros((GRID_W, BRANCH_W), F32)
        for h in range(N_HEADS):
            o = o + jnp.where(lane_q == h, o_all[h * GRID_W:(h + 1) * GRID_W, :], 0.0)
        o_ref[0, j * GRID_W:(j + 1) * GRID_W, :] = o


def _na_call(geo, nq, nk, nv, bias):
    b, l, _ = nq.shape
    blk = NA_RB * GRID_W
    full = pl.BlockSpec((1, l, BRANCH_W), lambda bb, i: (bb, 0, 0))
    return pl.pallas_call(
        functools.partial(_na_kernel, rows=geo.rows),
        grid=(b, l // blk),
        in_specs=[pl.BlockSpec((1, blk, BRANCH_W), lambda bb, i: (bb, i, 0)), full, full,
                  _const_spec(bias.shape)],
        out_specs=pl.BlockSpec((1, blk, BRANCH_W), lambda bb, i: (bb, i, 0)),
        out_shape=jax.ShapeDtypeStruct((b, l, BRANCH_W), F32),
        compiler_params=_cparams(("parallel", "arbitrary")),
        name="nbr_attn",
    )(nq, nk, nv, bias)


def _na_bias_table(rpb):
    w = np.arange(GRID_W)
    cs = np.clip(w - NA_COLS // 2, 0, GRID_W - NA_COLS)
    kc = np.arange(GRID_W)
    allowed = (kc[None, :] >= cs[:, None]) & (kc[None, :] < cs[:, None] + NA_COLS)
    dcol = np.clip(kc[None, :] - w[:, None] + (NA_COLS - 1), 0, 2 * NA_COLS - 2)
    delta = np.arange(NA_ROWS)
    wi = np.arange(NA_ROWS)
    drow = wi[None, :] - delta[:, None] + (NA_ROWS - 1)
    sel_r = jnp.asarray(drow[:, :, None] == np.arange(2 * NA_ROWS - 1), F32)
    sel_c = jnp.asarray(dcol[:, :, None] == np.arange(2 * NA_COLS - 1), F32)
    t = jnp.einsum('xia,hab,wkb->xhwik', sel_r, rpb.astype(F32), sel_c,
                   precision=lax.Precision.HIGHEST)
    t = jnp.where(jnp.asarray(allowed)[None, None, :, None, :], t, NEG)
    return t.reshape(NA_ROWS, N_HEADS * GRID_W, NA_WIN)


S5_W2 = 2 * S5_G * S5_P


def _s5_kernel(*refs, tc, nch, emit_y):
    if emit_y:
        su_ref, b_ref, a_ref, c_ref, init_ref, y_ref, fin_ref, bu_scr, st_scr = refs
    else:
        su_ref, b_ref, a_ref, init_ref, fin_ref, bu_scr, st_scr = refs
    d = pl.program_id(0)
    c = pl.program_id(2)
    half = S5_W2 // 2

    @pl.when(c == 0)
    def _():
        st_scr[...] = init_ref[0, 0]

    rh = tc * 4
    for r in range(2):
        bu_scr[r * rh:(r + 1) * rh, :] = jnp.dot(su_ref[0, r * rh:(r + 1) * rh, :].astype(BF16),
                                                 b_ref[0], preferred_element_type=F32)
    a = a_ref[0]
    ar = jnp.broadcast_to(a[:, :half], (8, half))
    ai = jnp.broadcast_to(a[:, half:], (8, half))

    def body(t, carry):
        sr, si = carry
        tt = jnp.where(d == 0, t, tc - 1 - t)
        off = pl.multiple_of(tt * 8, 8)
        br = bu_scr[pl.ds(off, 8), 0:half]
        bi = bu_scr[pl.ds(off, 8), half:S5_W2]
        nr = ar * sr - ai * si + br
        ni = ar * si + ai * sr + bi
        if emit_y:
            bu_scr[pl.ds(off, 8), 0:half] = nr
            bu_scr[pl.ds(off, 8), half:S5_W2] = ni
        return nr, ni

    sr, si = lax.fori_loop(0, tc, body, (st_scr[:, 0:half], st_scr[:, half:S5_W2]), unroll=8)
    st_scr[:, 0:half] = sr
    st_scr[:, half:S5_W2] = si
    if emit_y:
        for r in range(2):
            y_ref[0, 0, r * rh:(r + 1) * rh, :] = jnp.dot(
                bu_scr[r * rh:(r + 1) * rh, :].astype(BF16), c_ref[0], preferred_element_type=F32)

    @pl.when(c == nch - 1)
    def _():
        fin_ref[0, 0] = st_scr[...]


def _s5_call(geo, su, bcat, a, ccat, init, emit_y):
    lseg = geo.lseg
    tc = min(256, lseg)
    nch = lseg // tc
    su2 = su.reshape(2, lseg * 8, BRANCH_W)

    def chunk(d, c):
        return jnp.where(d == 0, c, nch - 1 - c)

    in_specs = [pl.BlockSpec((1, tc * 8, BRANCH_W), lambda d, hf, c: (hf, chunk(d, c), 0)),
                pl.BlockSpec((1, BRANCH_W, S5_W2), lambda d, hf, c: (d, 0, 0)),
                pl.BlockSpec((1, 1, S5_W2), lambda d, hf, c: (d, 0, 0))]
    args = [su2, bcat, a]
    if emit_y:
        in_specs.append(pl.BlockSpec((1, S5_W2, BRANCH_W), lambda d, hf, c: (d, 0, 0)))
        args.append(ccat)
    st_spec = pl.BlockSpec((1, 1, 8, S5_W2), lambda d, hf, c: (d, hf, 0, 0))
    in_specs.append(st_spec)
    args.append(init)
    st_shape = jax.ShapeDtypeStruct((2, 2, 8, S5_W2), F32)
    if emit_y:
        out_specs = [pl.BlockSpec((1, 1, tc * 8, BRANCH_W), lambda d, hf, c: (d, hf, chunk(d, c), 0)),
                     st_spec]
        out_shape = [jax.ShapeDtypeStruct((2, 2, lseg * 8, BRANCH_W), F32), st_shape]
    else:
        out_specs = [st_spec]
        out_shape = [st_shape]
    outs = pl.pallas_call(
        functools.partial(_s5_kernel, tc=tc, nch=nch, emit_y=emit_y),
        grid=(2, 2, nch),
        in_specs=in_specs, out_specs=out_specs, out_shape=out_shape,
        scratch_shapes=[pltpu.VMEM((tc * 8, S5_W2), F32), pltpu.VMEM((8, S5_W2), F32)],
        compiler_params=_cparams(("arbitrary", "arbitrary", "arbitrary")),
        name="s5_scan" if emit_y else "s5_endstate",
    )(*args)
    return outs


def _s5_carry_kernel(fin_ref, a_ref, init_ref, *, lseg):
    half = S5_W2 // 2
    for d in range(2):
        a = a_ref[d]
        pr, pi = a[:, :half], a[:, half:]
        for _ in range(int(math.log2(lseg))):
            pr, pi = pr * pr - pi * pi, 2.0 * pr * pi
        for hf in range(2):
            sr = jnp.zeros((1, half), F32)
            si = jnp.zeros((1, half), F32)
            order = range(8) if d == 0 else range(7, -1, -1)
            for s in order:
                init_ref[d, hf, s:s + 1, 0:half] = sr
                init_ref[d, hf, s:s + 1, half:S5_W2] = si
                fr = fin_ref[d, hf, s:s + 1, 0:half]
                fi = fin_ref[d, hf, s:s + 1, half:S5_W2]
                sr, si = pr * sr - pi * si + fr, pr * si + pi * sr + fi


def _s5_carry_call(geo, fin, a):
    return pl.pallas_call(
        functools.partial(_s5_carry_kernel, lseg=geo.lseg),
        out_shape=jax.ShapeDtypeStruct(fin.shape, F32),
        name="s5_carry",
    )(fin, a)


def _s5_mixer(geo, su, s5w):
    bcat, a, ccat = s5w
    zeros = jnp.zeros((2, 2, 8, S5_W2), F32)
    if geo.spb == 1:
        init = zeros
    else:
        assert geo.spb == 8
        (fin,) = _s5_call(geo, su, bcat, a, None, zeros, emit_y=False)
        init = _s5_carry_call(geo, fin, a)
    y, _ = _s5_call(geo, su, bcat, a, ccat, init, emit_y=True)
    return y.reshape(2, 2, geo.lseg, 8 * BRANCH_W)


def _s5_weights(a_re, a_im, log_dt, b_re, b_im, c_re, c_im):
    lam_re = jnp.minimum(a_re.astype(F32), -1e-4)
    lam_im = a_im.astype(F32)
    dt = jnp.exp(log_dt.astype(F32))[..., None]
    mag = jnp.exp(lam_re * dt)
    abar_re = mag * jnp.cos(lam_im * dt)
    abar_im = mag * jnp.sin(lam_im * dt)
    den = lam_re * lam_re + lam_im * lam_im
    f_re = ((abar_re - 1.0) * lam_re + abar_im * lam_im) / den
    f_im = (abar_im * lam_re - (abar_re - 1.0) * lam_im) / den
    br, bi = b_re.astype(F32), b_im.astype(F32)
    bb_re = f_re[..., None] * br - f_im[..., None] * bi
    bb_im = f_re[..., None] * bi + f_im[..., None] * br
    eye = jnp.eye(S5_G, dtype=F32)

    def bd_in(m):
        return jnp.einsum('dgph,gk->dghkp', m, eye).reshape(2, S5_G * S5_H, S5_G * S5_P)

    def bd_out(m):
        return jnp.einsum('dghp,gk->dgpkh', m, eye).reshape(2, S5_G * S5_P, S5_G * S5_H)

    bcat = jnp.concatenate([bd_in(bb_re), bd_in(bb_im)], axis=-1).astype(BF16)
    ccat = jnp.concatenate([bd_out(c_re.astype(F32)), -bd_out(c_im.astype(F32))], axis=1).astype(BF16)
    a = jnp.concatenate([abar_re.reshape(2, 1, -1), abar_im.reshape(2, 1, -1)], axis=-1)
    return bcat, a, ccat


def _ret_kernel(qf_ref, kf_ref, vf_ref, qb_ref, kb_ref, vb_ref, dec_ref, xi_ref, zeta_ref, cd_ref,
                of_ref, ob_ref, st_scr):
    i = pl.program_id(1)
    cr = qf_ref.shape[1]

    @pl.when(i == 0)
    def _():
        st_scr[...] = jnp.zeros(st_scr.shape, F32)

    lane = lax.broadcasted_iota(jnp.int32, (cr, BRANCH_W), 1) // HEAD_W
    blk = (lax.broadcasted_iota(jnp.int32, (BRANCH_W, BRANCH_W), 0) // HEAD_W
           == lax.broadcasted_iota(jnp.int32, (BRANCH_W, BRANCH_W), 1) // HEAD_W)
    dirs = ((qf_ref, kf_ref, vf_ref, of_ref), (qb_ref, kb_ref, vb_ref, ob_ref))
    for d, (q_ref, k_ref, v_ref, o_ref) in enumerate(dirs):
        q, k, v = q_ref[0], k_ref[0], v_ref[0]
        state = st_scr[d]
        o = jnp.dot(q, state.astype(BF16), preferred_element_type=F32) * xi_ref[d]
        for h in range(N_HEADS):
            qh = jnp.where(lane == h, q, jnp.zeros_like(q))
            vh = jnp.where(lane == h, v, jnp.zeros_like(v))
            s = lax.dot_general(qh, k, (((1,), (1,)), ((), ())), preferred_element_type=F32)
            s = s * dec_ref[d, h]
            o = o + jnp.dot(s.astype(BF16), vh, preferred_element_type=F32)
        o_ref[0] = o
        kz = (k.astype(F32) * zeta_ref[d]).T.astype(BF16)
        kv = jnp.dot(kz, v, preferred_element_type=F32)
        st_scr[d] = state * cd_ref[d] + jnp.where(blk, kv, 0.0)


def _ret_call(rq, rk, rv, tabs):
    b, l, _ = rq.shape
    cr = tabs[0].shape[-1]
    n = l // cr
    fwd = pl.BlockSpec((1, cr, BRANCH_W), lambda bb, i: (bb, i, 0))
    bwd = pl.BlockSpec((1, cr, BRANCH_W), lambda bb, i: (bb, n - 1 - i, 0))
    shp = jax.ShapeDtypeStruct((b, l, BRANCH_W), F32)
    return pl.pallas_call(
        _ret_kernel,
        grid=(b, n),
        in_specs=[fwd, fwd, fwd, bwd, bwd, bwd] + [_const_spec(t.shape) for t in tabs],
        out_specs=[fwd, bwd],
        out_shape=[shp, shp],
        scratch_shapes=[pltpu.VMEM((2, BRANCH_W, BRANCH_W), F32)],
        compiler_params=_cparams(("parallel", "arbitrary")),
        name="retention",
    )(rq, rk, rv, rq, rk, rv, *tabs)


def _ret_tables(ret_log_decay, cr):
    lg = -jnp.exp(ret_log_decay.astype(F32))
    idx = jnp.arange(cr, dtype=F32)
    rel = idx[:, None] - idx[None, :]
    m_f = rel >= 0
    m_b = rel < 0
    dec_f = jnp.where(m_f[None], jnp.exp(lg[0][:, None, None] * jnp.where(m_f, rel, 0.0)[None]), 0.0)
    dec_b = jnp.where(m_b[None], jnp.exp(lg[1][:, None, None] * jnp.where(m_b, -rel, 0.0)[None]), 0.0)
    dec = jnp.stack([dec_f, dec_b])
    lane_lg = jnp.repeat(lg, HEAD_W, axis=1)
    xi = jnp.stack([jnp.exp(lane_lg[0][None, :] * (idx + 1.0)[:, None]),
                    jnp.exp(lane_lg[1][None, :] * (cr - idx)[:, None])])
    zeta = jnp.stack([jnp.exp(lane_lg[0][None, :] * (cr - 1.0 - idx)[:, None]),
                      jnp.exp(lane_lg[1][None, :] * idx[:, None])])
    cd = jnp.exp(lane_lg * cr)[:, None, :]
    return dec, xi, zeta, cd


def _merge_kernel(x_ref, g_ref, sc_ref, sh_ref, g1_ref,
                  oaT_ref, ob_ref, su_ref, yf_ref, yb_ref, rf_ref, rb_ref, rg_ref,
                  s5d_ref, gluw_ref, glub_ref, rng_ref, ones_ref,
                  wg_ref, wb_ref, wo_ref, o_ref):
    tm = x_ref.shape[1]
    nh = 2 if tm % 256 == 0 else 1
    rh = tm // nh
    o_a_full = oaT_ref[0].T
    for hh in range(nh):
        rs = slice(hh * rh, (hh + 1) * rh)
        x = x_ref[0, rs, :]
        h = _norm_mod(x, g_ref[...], sc_ref[0], sh_ref[0]).astype(BF16)

        o_a = o_a_full[rs, :]
        o_b = ob_ref[0, rs, :]

        y = s5d_ref[...] * su_ref[0, rs, :] + yf_ref[0, 0, rs, :] + yb_ref[0, 0, rs, :]
        y = jax.nn.gelu(y)
        o_c = y * jax.nn.sigmoid(jnp.dot(y.astype(BF16), gluw_ref[...], preferred_element_type=F32)
                                 + glub_ref[...])

        z = rf_ref[0, rs, :] + rb_ref[0, rs, :]
        z2 = z * z
        hi = z2.astype(BF16)
        lo = (z2 - hi.astype(F32)).astype(BF16)
        ss = (jnp.dot(hi, ones_ref[...], preferred_element_type=F32)
              + jnp.dot(lo, ones_ref[...], preferred_element_type=F32))
        rg = rg_ref[0, rs, :]
        o_d = z * lax.rsqrt(ss * (1.0 / HEAD_W) + EPS) * rng_ref[...] * (rg * jax.nn.sigmoid(rg))

        merged = None
        for i, br in enumerate((o_a, o_b, o_c, o_d)):
            gate = jax.nn.sigmoid(jnp.dot(h, wg_ref[i], preferred_element_type=F32))
            t = gate * jnp.dot(br.astype(BF16), wb_ref[i], preferred_element_type=F32)
            merged = t if merged is None else merged + t
        out = jnp.dot(merged.astype(BF16), wo_ref[...], preferred_element_type=F32)
        o_ref[0, rs, :] = x + g1_ref[0] * out


def _merge_call(geo, x, g, sc, sh, g1, oaT, ob, su, y, rf, rb, rg, mw):
    b, l, d = x.shape
    tm = geo.tm
    tok = pl.BlockSpec((1, tm, BRANCH_W), lambda bb, i: (bb, i, 0))
    xs = pl.BlockSpec((1, tm, d), lambda bb, i: (bb, i, 0))
    vec = pl.BlockSpec((1, 1, d), lambda bb, i: (bb, 0, 0))
    su_spec = pl.BlockSpec((1, tm, BRANCH_W), lambda bb, i: geo.scan_index(bb, i))

    def y_spec(dd):
        return pl.BlockSpec((1, 1, tm, BRANCH_W), lambda bb, i: (dd,) + geo.scan_index(bb, i))

    s5_d, glu_w, glu_b, ret_g, ones_bd, wg, wb, wo = mw
    return pl.pallas_call(
        _merge_kernel,
        grid=(b, geo.nt),
        in_specs=[xs, _const_spec((1, d)), vec, vec, vec,
                  pl.BlockSpec((1, BRANCH_W, tm), lambda bb, i: (bb, 0, i)), tok,
                  su_spec, y_spec(0), y_spec(1), tok, tok, tok,
                  _const_spec(s5_d.shape), _const_spec(glu_w.shape), _const_spec(glu_b.shape),
                  _const_spec(ret_g.shape), _const_spec(ones_bd.shape),
                  _const_spec(wg.shape), _const_spec(wb.shape), _const_spec(wo.shape)],
        out_specs=xs,
        out_shape=jax.ShapeDtypeStruct(x.shape, F32),
        compiler_params=_cparams(("parallel", "parallel")),
        name="merge",
    )(x, g, sc, sh, g1, oaT, ob, su, y, y, rf, rb, rg, s5_d, glu_w, glu_b, ret_g, ones_bd, wg, wb, wo)


HALO = 8


def _ffn_kernel(xp_ref, x_ref, xn_ref, g_ref, sc_ref, sh_ref, g2_ref,
                wu_ref, wgt_ref, cw_ref, cb_ref, wd_ref, fg_ref, o_ref, a_scr, *, nt, final):
    i = pl.program_id(1)
    tm = x_ref.shape[1]
    g, sc, sh = g_ref[...], sc_ref[0], sh_ref[0]
    x = x_ref[0]
    h = _norm_mod(x, g, sc, sh)
    hp = _norm_mod(xp_ref[0], g, sc, sh) * (i > 0).astype(F32)
    hn = _norm_mod(xn_ref[0], g, sc, sh) * (i < nt - 1).astype(F32)
    h_ext = jnp.concatenate([hp, h, hn], axis=0).astype(BF16)
    a_scr[...] = jnp.dot(h_ext, wu_ref[...], preferred_element_type=F32)
    cw = cw_ref[...]
    a = (a_scr[pl.ds(HALO - 1, tm), :] * cw[0:1] + a_scr[pl.ds(HALO, tm), :] * cw[1:2]
         + a_scr[pl.ds(HALO + 1, tm), :] * cw[2:3] + cb_ref[...])
    gate = jnp.dot(h.astype(BF16), wgt_ref[...], preferred_element_type=F32)
    act = (jax.nn.gelu(a) * gate).astype(BF16)
    y = x + g2_ref[0] * jnp.dot(act, wd_ref[...], preferred_element_type=F32)
    if final:
        ms = jnp.mean(y * y, axis=-1, keepdims=True)
        y = y * lax.rsqrt(ms + EPS) * fg_ref[...]
    o_ref[0] = y


def _ffn_call(geo, x, g, sc, sh, g2, fw, final_g, final):
    b, l, d = x.shape
    tm = min(geo.tm, 512)
    nt = l // tm
    r = tm // HALO
    nh = l // HALO
    xs = pl.BlockSpec((1, tm, d), lambda bb, i: (bb, i, 0))
    prev = pl.BlockSpec((1, HALO, d), lambda bb, i: (bb, jnp.maximum(i * r - 1, 0), 0))
    nxt = pl.BlockSpec((1, HALO, d), lambda bb, i: (bb, jnp.minimum((i + 1) * r, nh - 1), 0))
    vec = pl.BlockSpec((1, 1, d), lambda bb, i: (bb, 0, 0))
    wu, wgt, cw, cb, wd = fw
    once = pl.Buffered(1)

    def wspec(shape):
        return pl.BlockSpec(shape, lambda *_: (0,) * len(shape), pipeline_mode=once)

    return pl.pallas_call(
        functools.partial(_ffn_kernel, nt=nt, final=final),
        grid=(b, nt),
        in_specs=[prev, xs, nxt, _const_spec((1, d)), vec, vec, vec,
                  wspec(wu.shape), wspec(wgt.shape), _const_spec(cw.shape), _const_spec(cb.shape),
                  wspec(wd.shape), _const_spec((1, d))],
        out_specs=xs,
        out_shape=jax.ShapeDtypeStruct(x.shape, F32),
        scratch_shapes=[pltpu.VMEM((tm + 2 * HALO, D_FF), F32)],
        compiler_params=_cparams(("parallel", "parallel")),
        name="conv_ffn",
    )(x, x, x, g, sc, sh, g2, wu, wgt, cw, cb, wd, final_g)


RET_CHUNK = 256


def _layer_weights(l, p):
    d = p['w_in'].shape[1]
    ones_bd = jnp.asarray(np.kron(np.eye(N_HEADS), np.ones((HEAD_W, HEAD_W))), BF16)
    return dict(
        norm1_g=p['norm1_g'][l].reshape(1, d),
        norm2_g=p['norm2_g'][l].reshape(1, d),
        w_in=p['w_in'][l].astype(BF16),
        da_lambda=p['da_lambda'][l].astype(F32),
        da_subln_g=p['da_subln_g'][l].astype(F32),
        na_bias=_na_bias_table(p['na_rpb'][l]),
        gsum=jnp.asarray(np.kron(np.eye(2 * N_HEADS), np.ones((DA_DIM, DA_DIM))), BF16),
        s5=_s5_weights(p['s5_a_re'][l], p['s5_a_im'][l], p['s5_log_dt'][l], p['s5_b_re'][l],
                       p['s5_b_im'][l], p['s5_c_re'][l], p['s5_c_im'][l]),
        ret=_ret_tables(p['ret_log_decay'][l], RET_CHUNK),
        merge=(p['s5_d'][l].reshape(1, BRANCH_W).astype(F32),
               p['s5_glu_w'][l].astype(BF16),
               p['s5_glu_b'][l].reshape(1, BRANCH_W).astype(F32),
               jnp.tile(p['ret_norm_g'][l].astype(F32), N_HEADS).reshape(1, BRANCH_W),
               ones_bd,
               p['w_branch_gate'][l].astype(BF16),
               p['w_branch'][l].astype(BF16),
               p['w_out'][l].astype(BF16)),
        ffn=(p['ffn_w_up'][l].astype(BF16),
             p['ffn_w_gate'][l].astype(BF16),
             p['ffn_conv_w'][l].reshape(3, D_FF).astype(F32),
             p['ffn_conv_b'][l].reshape(1, D_FF).astype(F32),
             p['ffn_w_down'][l].astype(BF16)),
    )


def _layer(geo, x, mod, lw, lam_init, rope, final_g, final):
    b, l, d = x.shape
    sh1, sc1, g1, sh2, sc2, g2 = [mod[:, k * d:(k + 1) * d].reshape(b, 1, d) for k in range(6)]
    (qT, k, kn, vT, nq, nk, nv, su, rq, rk, rv, rg) = _proj_call(
        geo, x, lw['norm1_g'], sc1, sh1, lw['w_in'], rope, lw['gsum'])
    oaT = _da_call(qT, k, vT, kn, lw['da_lambda'], lw['da_subln_g'], lam_init)
    ob = _na_call(geo, nq, nk, nv, lw['na_bias'])
    y = _s5_mixer(geo, su, lw['s5'])
    rf, rb = _ret_call(rq, rk, rv, lw['ret'])
    x = _merge_call(geo, x, lw['norm1_g'], sc1, sh1, g1, oaT, ob, su, y, rf, rb, rg, lw['merge'])
    return _ffn_call(geo, x, lw['norm2_g'], sc2, sh2, g2, lw['ffn'], final_g, final)


def kernel(x_prompt, x_sample, c_prompt, c_sample, norm1_g, norm2_g, ada_w, ada_b, w_in, da_lambda, da_subln_g, na_rpb, s5_a_re, s5_a_im, s5_log_dt, s5_b_re, s5_b_im, s5_c_re, s5_c_im, s5_d, s5_glu_w, s5_glu_b, ret_log_decay, ret_norm_g, w_branch, w_branch_gate, w_out, ffn_w_up, ffn_w_gate, ffn_conv_w, ffn_conv_b, ffn_w_down, final_norm_g):
    p = dict(norm1_g=norm1_g, norm2_g=norm2_g, w_in=w_in, da_lambda=da_lambda,
             da_subln_g=da_subln_g, na_rpb=na_rpb, s5_a_re=s5_a_re, s5_a_im=s5_a_im,
             s5_log_dt=s5_log_dt, s5_b_re=s5_b_re, s5_b_im=s5_b_im, s5_c_re=s5_c_re,
             s5_c_im=s5_c_im, s5_d=s5_d, s5_glu_w=s5_glu_w, s5_glu_b=s5_glu_b,
             ret_log_decay=ret_log_decay, ret_norm_g=ret_norm_g, w_branch=w_branch,
             w_branch_gate=w_branch_gate, w_out=w_out, ffn_w_up=ffn_w_up, ffn_w_gate=ffn_w_gate,
             ffn_conv_w=ffn_conv_w, ffn_conv_b=ffn_conv_b, ffn_w_down=ffn_w_down)
    depth = w_in.shape[0]
    d = w_in.shape[1]
    nbp = c_prompt.shape[0]
    mod = _mod_call(jnp.concatenate([c_prompt, c_sample], axis=0), ada_w, ada_b)
    lws = [_layer_weights(l, p) for l in range(depth)]
    final_g = final_norm_g.reshape(1, d).astype(F32)
    outs = []
    for x, lo in ((x_prompt, 0), (x_sample, nbp)):
        b, l, _ = x.shape
        geo = _Geom(b, l)
        rope = _rope_tables(l)
        for li in range(depth):
            lam_init = 0.8 - 0.6 * math.exp(-0.3 * li)
            x = _layer(geo, x, mod[li, lo:lo + b], lws[li], lam_init, rope, final_g,
                       final=(li == depth - 1))
        outs.append(x)
    return tuple(outs)
```

```python
import functools
import math

import numpy as np
import jax
import jax.numpy as jnp
from jax import lax
from jax.experimental import pallas as pl
from jax.experimental.pallas import tpu as pltpu

F32 = jnp.float32
BF16 = jnp.bfloat16

EPS = 1e-6
ROPE_THETA = 500000.0
BRANCH_W = 256
N_SEG = 11
DA_DIM = 32
HEAD_W = 64
N_HEADS = 4
GRID_W = 64
NA_ROWS = 8
NA_COLS = 16
S5_G = 16
S5_P = 64
S5_H = 16
S5_NSEQ = 16
D_FF = 2816
NEG = -1e30
LOG2E = 1.4426950408889634

VMEM_LIMIT = 56 * 1024 * 1024


def _cparams(sem):
    return pltpu.CompilerParams(dimension_semantics=sem, vmem_limit_bytes=VMEM_LIMIT)


def _const_spec(shape):
    nd = len(shape)
    return pl.BlockSpec(shape, lambda *_: (0,) * nd)


def _norm_mod(x, g, sc, sh):
    ms = jnp.mean(x * x, axis=-1, keepdims=True)
    y = x * lax.rsqrt(ms + EPS) * g
    return y * (1.0 + sc) + sh


def _mod_kernel(c_ref, w_ref, b_ref, o_ref):
    c = c_ref[...]
    s = c * jax.nn.sigmoid(c)
    o_ref[0] = jnp.dot(s.astype(BF16), w_ref[0].astype(BF16),
                       preferred_element_type=F32) + b_ref[0]


def _mod_call(c_all, ada_w, ada_b):
    depth, d, n6 = ada_w.shape
    nb = c_all.shape[0]
    tn = 1536
    return pl.pallas_call(
        _mod_kernel,
        grid=(depth, n6 // tn),
        in_specs=[pl.BlockSpec((nb, d), lambda l, j: (0, 0)),
                  pl.BlockSpec((1, d, tn), lambda l, j: (l, 0, j)),
                  pl.BlockSpec((1, 1, tn), lambda l, j: (l, 0, j))],
        out_specs=pl.BlockSpec((1, nb, tn), lambda l, j: (l, 0, j)),
        out_shape=jax.ShapeDtypeStruct((depth, nb, n6), F32),
        compiler_params=_cparams(("arbitrary", "arbitrary")),
        name="adaln_mod",
    )(c_all, ada_w, ada_b.reshape(depth, 1, n6))


class _Geom:
    def __init__(self, b, l):
        assert S5_NSEQ % b == 0
        self.b, self.l = b, l
        self.spb = S5_NSEQ // b
        assert l % self.spb == 0
        self.lseg = l // self.spb
        assert self.lseg & (self.lseg - 1) == 0
        self.tm = min(512, self.lseg)
        assert self.lseg % self.tm == 0
        self.tps = self.lseg // self.tm
        self.nt = l // self.tm
        self.rows = l // GRID_W
        assert self.rows >= NA_ROWS and l % (NA_RB * GRID_W) == 0

    def scan_index(self, b, i):
        n = b * self.spb + i // self.tps
        return n // 8, i % self.tps, n % 8


def _proj_kernel(x_ref, g_ref, sc_ref, sh_ref, w_ref, cos_ref, sina_ref, sinb_ref, gsum_ref,
                 qT_ref, k_ref, kn_ref, vT_ref, nq_ref, nk_ref, nv_ref, su_ref,
                 rq_ref, rk_ref, rv_ref, rg_ref):
    tm = x_ref.shape[1]
    nh = 2 if tm % 512 == 0 else 1
    rh = tm // nh
    qscale = DA_DIM ** -0.5 * LOG2E
    k_sq_max = None
    for hh in range(nh):
        rs = slice(hh * rh, (hh + 1) * rh)
        h = _norm_mod(x_ref[0, rs, :], g_ref[...], sc_ref[0], sh_ref[0]).astype(BF16)
        p_all = jnp.dot(h, w_ref[...], preferred_element_type=F32)

        def seg(j):
            return p_all[:, j * BRANCH_W:(j + 1) * BRANCH_W]

        cos, sina, sinb = cos_ref[rs, :], sina_ref[rs, :], sinb_ref[rs, :]

        def rot(p):
            halves = []
            for s in range(2):
                ph = p[:, s * 128:(s + 1) * 128]
                halves.append(ph * cos + pltpu.roll(ph, 124, 1) * sina + pltpu.roll(ph, 4, 1) * sinb)
            return jnp.concatenate(halves, axis=1)

        qT_ref[0, :, rs] = (rot(seg(0)) * qscale).T.astype(BF16)
        k_bf = rot(seg(1)).astype(BF16)
        k_ref[0, rs, :] = k_bf
        k_f = k_bf.astype(F32)
        k_sq = jnp.max(jnp.dot((k_f * k_f).astype(BF16), gsum_ref[...], preferred_element_type=F32),
                       axis=0, keepdims=True)
        k_sq_max = k_sq if k_sq_max is None else jnp.maximum(k_sq_max, k_sq)
        vT_ref[0, :, rs] = seg(2).T.astype(BF16)
        nq_ref[0, rs, :] = (seg(3) * HEAD_W ** -0.5).astype(BF16)
        nk_ref[0, rs, :] = seg(4).astype(BF16)
        nv_ref[0, rs, :] = seg(5).astype(BF16)
        su_ref[0, rs, :] = seg(6)
        rq_ref[0, rs, :] = seg(7).astype(BF16)
        rk_ref[0, rs, :] = (seg(8) * HEAD_W ** -0.5).astype(BF16)
        rv_ref[0, rs, :] = seg(9).astype(BF16)
        rg_ref[0, rs, :] = seg(10)
    kn_ref[0, 0] = k_sq_max


def _proj_call(geo, x, g, sc, sh, w_in, rope, gsum):
    b, l, d = x.shape
    tm = geo.tm
    tok = pl.BlockSpec((1, tm, BRANCH_W), lambda bb, i: (bb, i, 0))
    tokT = pl.BlockSpec((1, BRANCH_W, tm), lambda bb, i: (bb, 0, i))
    vec = pl.BlockSpec((1, 1, d), lambda bb, i: (bb, 0, 0))
    tab = pl.BlockSpec((tm, 128), lambda bb, i: (i, 0))
    su_spec = pl.BlockSpec((1, tm, BRANCH_W), lambda bb, i: geo.scan_index(bb, i))
    bl = jax.ShapeDtypeStruct((b, l, BRANCH_W), BF16)
    blT = jax.ShapeDtypeStruct((b, BRANCH_W, l), BF16)
    blf = jax.ShapeDtypeStruct((b, l, BRANCH_W), F32)
    su_shape = jax.ShapeDtypeStruct((2, geo.lseg, 8 * BRANCH_W), F32)
    kn_spec = pl.BlockSpec((1, 1, 1, BRANCH_W), lambda bb, i: (bb, i, 0, 0))
    kn_shape = jax.ShapeDtypeStruct((b, geo.nt, 1, BRANCH_W), F32)
    return pl.pallas_call(
        _proj_kernel,
        grid=(b, geo.nt),
        in_specs=[pl.BlockSpec((1, tm, d), lambda bb, i: (bb, i, 0)),
                  _const_spec((1, d)), vec, vec,
                  _const_spec((d, N_SEG * BRANCH_W)), tab, tab, tab, _const_spec(gsum.shape)],
        out_specs=[tokT, tok, kn_spec, tokT, tok, tok, tok, su_spec, tok, tok, tok, tok],
        out_shape=[blT, bl, kn_shape, blT, bl, bl, bl, su_shape, bl, bl, bl, blf],
        compiler_params=_cparams(("parallel", "parallel")),
        name="proj",
    )(x, g, sc, sh, w_in, *rope, gsum)


def _rope_tables(l):
    half = DA_DIM // 4 // 2
    inv_freq = jnp.power(jnp.float32(ROPE_THETA), -jnp.arange(half, dtype=F32) / half)
    ang = jnp.arange(l).astype(F32)[:, None] * inv_freq[None, :]
    cos, sin = jnp.cos(ang), jnp.sin(ang)
    zeros = jnp.zeros((l, half), F32)
    rest = DA_DIM - 2 * half

    def lanes(first, second, fill):
        comp = jnp.concatenate([first, second, jnp.full((l, rest), fill, F32)], axis=1)
        return jnp.tile(comp, (1, 128 // DA_DIM))

    return lanes(cos, cos, 1.0), lanes(-sin, zeros, 0.0), lanes(zeros, sin, 0.0)


DA_TQ = 1024
DA_TK = 1024
DA_SEED = 128
DA_SLACK = 60.0
DA_BOUND_PAD = 1.02


def _da_kernel(lam_ref, g_ref, kn_ref, qT_ref, k_ref, vT_ref, o_ref,
               qm_scr, m_scr, l_scr, acc_scr, mode_ref, *, lam_init, nkv):
    kv = pl.program_id(2)
    n_sets = 2 * N_HEADS

    @pl.when(kv == 0)
    def _():
        qT = qT_ref[0]
        row = lax.broadcasted_iota(jnp.int32, qT.shape, 0)
        k_seed = k_ref[0, 0:DA_SEED, :]
        qf = qT.astype(F32)
        q2 = qf * qf
        k2max = jnp.max(jnp.max(kn_ref[0], axis=0), axis=-1, keepdims=True)
        k_bound = jnp.sqrt(k2max * DA_BOUND_PAD)
        worst = None
        for hc in range(n_sets):
            keep = (row >= hc * DA_DIM) & (row < (hc + 1) * DA_DIM)
            qm = jnp.where(keep, qT, jnp.zeros_like(qT))
            qm_scr[hc] = qm
            m0 = jnp.max(jnp.dot(k_seed, qm, preferred_element_type=F32), axis=0, keepdims=True)
            m_scr[hc] = m0
            q_norm = jnp.sqrt(jnp.sum(q2[hc * DA_DIM:(hc + 1) * DA_DIM, :], axis=0, keepdims=True))
            over = q_norm * k_bound - m0
            worst = over if worst is None else jnp.maximum(worst, over)
        mode_ref[0] = (jnp.max(worst) <= DA_SLACK).astype(jnp.int32)
        l_scr[...] = jnp.zeros(l_scr.shape, F32)
        acc_scr[...] = jnp.zeros(acc_scr.shape, F32)

    kb = k_ref[0]
    fixed_shift = mode_ref[0] == 1

    @pl.when(fixed_shift)
    def _():
        for hc in range(n_sets):
            h = hc // 2
            s_t = jnp.dot(kb, qm_scr[hc], preferred_element_type=F32)
            p = jnp.exp2(s_t - m_scr[hc])
            l_scr[hc] = l_scr[hc] + jnp.sum(p, axis=0, keepdims=True)
            acc_scr[hc] = acc_scr[hc] + jnp.dot(
                vT_ref[0, h * HEAD_W:(h + 1) * HEAD_W, :], p.astype(BF16), preferred_element_type=F32)

    @pl.when(jnp.logical_not(fixed_shift))
    def _():
        for hc in range(n_sets):
            h = hc // 2
            s_t = jnp.dot(kb, qm_scr[hc], preferred_element_type=F32)
            m_old = m_scr[hc]
            m_new = jnp.maximum(m_old, jnp.max(s_t, axis=0, keepdims=True))
            p = jnp.exp2(s_t - m_new)
            alpha = jnp.exp2(m_old - m_new)
            l_scr[hc] = alpha * l_scr[hc] + jnp.sum(p, axis=0, keepdims=True)
            pv = jnp.dot(vT_ref[0, h * HEAD_W:(h + 1) * HEAD_W, :], p.astype(BF16),
                         preferred_element_type=F32)
            acc_scr[hc] = alpha * acc_scr[hc] + pv
            m_scr[hc] = m_new

    @pl.when(kv == nkv - 1)
    def _():
        lp = lam_ref[...]
        lam = (jnp.exp(jnp.sum(lp[0:1] * lp[1:2], axis=-1, keepdims=True))
               - jnp.exp(jnp.sum(lp[2:3] * lp[3:4], axis=-1, keepdims=True)) + lam_init)
        for h in range(N_HEADS):
            o1 = acc_scr[2 * h] / l_scr[2 * h]
            o2 = acc_scr[2 * h + 1] / l_scr[2 * h + 1]
            o = o1 - lam * o2
            ms = jnp.mean(o * o, axis=0, keepdims=True)
            y = o * lax.rsqrt(ms + EPS) * g_ref[h * HEAD_W:(h + 1) * HEAD_W, :]
            o_ref[0, h * HEAD_W:(h + 1) * HEAD_W, :] = y * (1.0 - lam_init)


def _da_call(qT, k, vT, kn, da_lambda, subln_g, lam_init):
    b, _, l = qT.shape
    tq = min(DA_TQ, l)
    tk = min(DA_TK, l)
    nkv = l // tk
    g_col = jnp.tile(subln_g, N_HEADS).reshape(BRANCH_W, 1)
    return pl.pallas_call(
        functools.partial(_da_kernel, lam_init=lam_init, nkv=nkv),
        grid=(b, l // tq, nkv),
        in_specs=[_const_spec((4, DA_DIM)), _const_spec((BRANCH_W, 1)),
                  pl.BlockSpec((1,) + kn.shape[1:], lambda bb, i, j: (bb, 0, 0, 0)),
                  pl.BlockSpec((1, BRANCH_W, tq), lambda bb, i, j: (bb, 0, i)),
                  pl.BlockSpec((1, tk, BRANCH_W), lambda bb, i, j: (bb, j, 0)),
                  pl.BlockSpec((1, BRANCH_W, tk), lambda bb, i, j: (bb, 0, j))],
        out_specs=pl.BlockSpec((1, BRANCH_W, tq), lambda bb, i, j: (bb, 0, i)),
        out_shape=jax.ShapeDtypeStruct((b, BRANCH_W, l), F32),
        scratch_shapes=[pltpu.VMEM((2 * N_HEADS, BRANCH_W, tq), BF16),
                        pltpu.VMEM((2 * N_HEADS, 1, tq), F32),
                        pltpu.VMEM((2 * N_HEADS, 1, tq), F32),
                        pltpu.VMEM((2 * N_HEADS, HEAD_W, tq), F32),
                        pltpu.SMEM((1,), jnp.int32)],
        compiler_params=_cparams(("parallel", "parallel", "arbitrary")),
        name="diff_attn",
    )(da_lambda, g_col, kn, qT, k, vT)


NA_RB = 16
NA_WIN = NA_ROWS * GRID_W


def _na_kernel(q_ref, k_ref, v_ref, bias_ref, o_ref, *, rows):
    i = pl.program_id(1)
    lane_q = lax.broadcasted_iota(jnp.int32, (GRID_W, BRANCH_W), 1) // HEAD_W
    for j in range(NA_RB):
        r = i * NA_RB + j
        rs = jnp.clip(r - NA_ROWS // 2, 0, rows - NA_ROWS)
        start = pl.multiple_of(rs * GRID_W, GRID_W)
        kw = k_ref[0, pl.ds(start, NA_WIN), :]
        vw = v_ref[0, pl.ds(start, NA_WIN), :]
        q = q_ref[0, j * GRID_W:(j + 1) * GRID_W, :]
        qm = jnp.concatenate(
            [jnp.where(lane_q == h, q, jnp.zeros_like(q)) for h in range(N_HEADS)], axis=0)
        s = lax.dot_general(qm, kw, (((1,), (1,)), ((), ())), preferred_element_type=F32)
        s = s + bias_ref[r - rs]
        m = jnp.max(s, axis=-1, keepdims=True)
        e = jnp.exp(s - m)
        den = jnp.sum(e, axis=-1, keepdims=True)
        o_all = jnp.dot(e.astype(BF16), vw, preferred_element_type=F32) / den
        o = jnp.zeros((GRID_W, BRANCH_W), F32)
        for h in range(N_HEADS):
            o = o + jnp.where(lane_q == h, o_all[h * GRID_W:(h + 1) * GRID_W, :], 0.0)
        o_ref[0, j * GRID_W:(j + 1) * GRID_W, :] = o


def _na_call(geo, nq, nk, nv, bias):
    b, l, _ = nq.shape
    blk = NA_RB * GRID_W
    full = pl.BlockSpec((1, l, BRANCH_W), lambda bb, i: (bb, 0, 0))
    return pl.pallas_call(
        functools.partial(_na_kernel, rows=geo.rows),
        grid=(b, l // blk),
        in_specs=[pl.BlockSpec((1, blk, BRANCH_W), lambda bb, i: (bb, i, 0)), full, full,
                  _const_spec(bias.shape)],
        out_specs=pl.BlockSpec((1, blk, BRANCH_W), lambda bb, i: (bb, i, 0)),
        out_shape=jax.ShapeDtypeStruct((b, l, BRANCH_W), F32),
        compiler_params=_cparams(("parallel", "arbitrary")),
        name="nbr_attn",
    )(nq, nk, nv, bias)


def _na_bias_table(rpb):
    w = np.arange(GRID_W)
    cs = np.clip(w - NA_COLS // 2, 0, GRID_W - NA_COLS)
    kc = np.arange(GRID_W)
    allowed = (kc[None, :] >= cs[:, None]) & (kc[None, :] < cs[:, None] + NA_COLS)
    dcol = np.clip(kc[None, :] - w[:, None] + (NA_COLS - 1), 0, 2 * NA_COLS - 2)
    delta = np.arange(NA_ROWS)
    wi = np.arange(NA_ROWS)
    drow = wi[None, :] - delta[:, None] + (NA_ROWS - 1)
    sel_r = jnp.asarray(drow[:, :, None] == np.arange(2 * NA_ROWS - 1), F32)
    sel_c = jnp.asarray(dcol[:, :, None] == np.arange(2 * NA_COLS - 1), F32)
    t = jnp.einsum('xia,hab,wkb->xhwik', sel_r, rpb.astype(F32), sel_c,
                   precision=lax.Precision.HIGHEST)
    t = jnp.where(jnp.asarray(allowed)[None, None, :, None, :], t, NEG)
    return t.reshape(NA_ROWS, N_HEADS * GRID_W, NA_WIN)


S5_W2 = 2 * S5_G * S5_P


def _s5_kernel(*refs, tc, nch, emit_y):
    if emit_y:
        su_ref, b_ref, a_ref, c_ref, init_ref, y_ref, fin_ref, bu_scr, st_scr = refs
    else:
        su_ref, b_ref, a_ref, init_ref, fin_ref, bu_scr, st_scr = refs
    d = pl.program_id(0)
    c = pl.program_id(2)
    half = S5_W2 // 2

    @pl.when(c == 0)
    def _():
        st_scr[...] = init_ref[0, 0]

    rh = tc * 4
    for r in range(2):
        bu_scr[r * rh:(r + 1) * rh, :] = jnp.dot(su_ref[0, r * rh:(r + 1) * rh, :].astype(BF16),
                                                 b_ref[0], preferred_element_type=F32)
    a = a_ref[0]
    ar = jnp.broadcast_to(a[:, :half], (8, half))
    ai = jnp.broadcast_to(a[:, half:], (8, half))

    def body(t, carry):
        sr, si = carry
        tt = jnp.where(d == 0, t, tc - 1 - t)
        off = pl.multiple_of(tt * 8, 8)
        br = bu_scr[pl.ds(off, 8), 0:half]
        bi = bu_scr[pl.ds(off, 8), half:S5_W2]
        nr = ar * sr - ai * si + br
        ni = ar * si + ai * sr + bi
        if emit_y:
            bu_scr[pl.ds(off, 8), 0:half] = nr
            bu_scr[pl.ds(off, 8), half:S5_W2] = ni
        return nr, ni

    sr, si = lax.fori_loop(0, tc, body, (st_scr[:, 0:half], st_scr[:, half:S5_W2]), unroll=8)
    st_scr[:, 0:half] = sr
    st_scr[:, half:S5_W2] = si
    if emit_y:
        for r in range(2):
            y_ref[0, 0, r * rh:(r + 1) * rh, :] = jnp.dot(
                bu_scr[r * rh:(r + 1) * rh, :].astype(BF16), c_ref[0], preferred_element_type=F32)

    @pl.when(c == nch - 1)
    def _():
        fin_ref[0, 0] = st_scr[...]


def _s5_call(geo, su, bcat, a, ccat, init, emit_y):
    lseg = geo.lseg
    tc = min(256, lseg)
    nch = lseg // tc
    su2 = su.reshape(2, lseg * 8, BRANCH_W)

    def chunk(d, c):
        return jnp.where(d == 0, c, nch - 1 - c)

    in_specs = [pl.BlockSpec((1, tc * 8, BRANCH_W), lambda d, hf, c: (hf, chunk(d, c), 0)),
                pl.BlockSpec((1, BRANCH_W, S5_W2), lambda d, hf, c: (d, 0, 0)),
                pl.BlockSpec((1, 1, S5_W2), lambda d, hf, c: (d, 0, 0))]
    args = [su2, bcat, a]
    if emit_y:
        in_specs.append(pl.BlockSpec((1, S5_W2, BRANCH_W), lambda d, hf, c: (d, 0, 0)))
        args.append(ccat)
    st_spec = pl.BlockSpec((1, 1, 8, S5_W2), lambda d, hf, c: (d, hf, 0, 0))
    in_specs.append(st_spec)
    args.append(init)
    st_shape = jax.ShapeDtypeStruct((2, 2, 8, S5_W2), F32)
    if emit_y:
        out_specs = [pl.BlockSpec((1, 1, tc * 8, BRANCH_W), lambda d, hf, c: (d, hf, chunk(d, c), 0)),
                     st_spec]
        out_shape = [jax.ShapeDtypeStruct((2, 2, lseg * 8, BRANCH_W), F32), st_shape]
    else:
        out_specs = [st_spec]
        out_shape = [st_shape]
    outs = pl.pallas_call(
        functools.partial(_s5_kernel, tc=tc, nch=nch, emit_y=emit_y),
        grid=(2, 2, nch),
        in_specs=in_specs, out_specs=out_specs, out_shape=out_shape,
        scratch_shapes=[pltpu.VMEM((tc * 8, S5_W2), F32), pltpu.VMEM((8, S5_W2), F32)],
        compiler_params=_cparams(("arbitrary", "arbitrary", "arbitrary")),
        name="s5_scan" if emit_y else "s5_endstate",
    )(*args)
    return outs


def _s5_carry_kernel(fin_ref, a_ref, init_ref, *, lseg):
    half = S5_W2 // 2
    for d in range(2):
        a = a_ref[d]
        pr, pi = a[:, :half], a[:, half:]
        for _ in range(int(math.log2(lseg))):
            pr, pi = pr * pr - pi * pi, 2.0 * pr * pi
        for hf in range(2):
            sr = jnp.zeros((1, half), F32)
            si = jnp.zeros((1, half), F32)
            order = range(8) if d == 0 else range(7, -1, -1)
            for s in order:
                init_ref[d, hf, s:s + 1, 0:half] = sr
                init_ref[d, hf, s:s + 1, half:S5_W2] = si
                fr = fin_ref[d, hf, s:s + 1, 0:half]
                fi = fin_ref[d, hf, s:s + 1, half:S5_W2]
                sr, si = pr * sr - pi * si + fr, pr * si + pi * sr + fi


def _s5_carry_call(geo, fin, a):
    return pl.pallas_call(
        functools.partial(_s5_carry_kernel, lseg=geo.lseg),
        out_shape=jax.ShapeDtypeStruct(fin.shape, F32),
        name="s5_carry",
    )(fin, a)


def _s5_mixer(geo, su, s5w):
    bcat, a, ccat = s5w
    zeros = jnp.zeros((2, 2, 8, S5_W2), F32)
    if geo.spb == 1:
        init = zeros
    else:
        assert geo.spb == 8
        (fin,) = _s5_call(geo, su, bcat, a, None, zeros, emit_y=False)
        init = _s5_carry_call(geo, fin, a)
    y, _ = _s5_call(geo, su, bcat, a, ccat, init, emit_y=True)
    return y.reshape(2, 2, geo.lseg, 8 * BRANCH_W)


def _s5_weights(a_re, a_im, log_dt, b_re, b_im, c_re, c_im):
    lam_re = jnp.minimum(a_re.astype(F32), -1e-4)
    lam_im = a_im.astype(F32)
    dt = jnp.exp(log_dt.astype(F32))[..., None]
    mag = jnp.exp(lam_re * dt)
    abar_re = mag * jnp.cos(lam_im * dt)
    abar_im = mag * jnp.sin(lam_im * dt)
    den = lam_re * lam_re + lam_im * lam_im
    f_re = ((abar_re - 1.0) * lam_re + abar_im * lam_im) / den
    f_im = (abar_im * lam_re - (abar_re - 1.0) * lam_im) / den
    br, bi = b_re.astype(F32), b_im.astype(F32)
    bb_re = f_re[..., None] * br - f_im[..., None] * bi
    bb_im = f_re[..., None] * bi + f_im[..., None] * br
    eye = jnp.eye(S5_G, dtype=F32)

    def bd_in(m):
        return jnp.einsum('dgph,gk->dghkp', m, eye).reshape(2, S5_G * S5_H, S5_G * S5_P)

    def bd_out(m):
        return jnp.einsum('dghp,gk->dgpkh', m, eye).reshape(2, S5_G * S5_P, S5_G * S5_H)

    bcat = jnp.concatenate([bd_in(bb_re), bd_in(bb_im)], axis=-1).astype(BF16)
    ccat = jnp.concatenate([bd_out(c_re.astype(F32)), -bd_out(c_im.astype(F32))], axis=1).astype(BF16)
    a = jnp.concatenate([abar_re.reshape(2, 1, -1), abar_im.reshape(2, 1, -1)], axis=-1)
    return bcat, a, ccat


def _ret_kernel(qf_ref, kf_ref, vf_ref, qb_ref, kb_ref, vb_ref, dec_ref, xi_ref, zeta_ref, cd_ref,
                of_ref, ob_ref, st_scr):
    i = pl.program_id(1)
    cr = qf_ref.shape[1]

    @pl.when(i == 0)
    def _():
        st_scr[...] = jnp.zeros(st_scr.shape, F32)

    lane = lax.broadcasted_iota(jnp.int32, (cr, BRANCH_W), 1) // HEAD_W
    blk = (lax.broadcasted_iota(jnp.int32, (BRANCH_W, BRANCH_W), 0) // HEAD_W
           == lax.broadcasted_iota(jnp.int32, (BRANCH_W, BRANCH_W), 1) // HEAD_W)
    dirs = ((qf_ref, kf_ref, vf_ref, of_ref), (qb_ref, kb_ref, vb_ref, ob_ref))
    for d, (q_ref, k_ref, v_ref, o_ref) in enumerate(dirs):
        q, k, v = q_ref[0], k_ref[0], v_ref[0]
        state = st_scr[d]
        o = jnp.dot(q, state.astype(BF16), preferred_element_type=F32) * xi_ref[d]
        for h in range(N_HEADS):
            qh = jnp.where(lane == h, q, jnp.zeros_like(q))
            vh = jnp.where(lane == h, v, jnp.zeros_like(v))
            s = lax.dot_general(qh, k, (((1,), (1,)), ((), ())), preferred_element_type=F32)
            s = s * dec_ref[d, h]
            o = o + jnp.dot(s.astype(BF16), vh, preferred_element_type=F32)
        o_ref[0] = o
        kz = (k.astype(F32) * zeta_ref[d]).T.astype(BF16)
        kv = jnp.dot(kz, v, preferred_element_type=F32)
        st_scr[d] = state * cd_ref[d] + jnp.where(blk, kv, 0.0)


def _ret_call(rq, rk, rv, tabs):
    b, l, _ = rq.shape
    cr = tabs[0].shape[-1]
    n = l // cr
    fwd = pl.BlockSpec((1, cr, BRANCH_W), lambda bb, i: (bb, i, 0))
    bwd = pl.BlockSpec((1, cr, BRANCH_W), lambda bb, i: (bb, n - 1 - i, 0))
    shp = jax.ShapeDtypeStruct((b, l, BRANCH_W), F32)
    return pl.pallas_call(
        _ret_kernel,
        grid=(b, n),
        in_specs=[fwd, fwd, fwd, bwd, bwd, bwd] + [_const_spec(t.shape) for t in tabs],
        out_specs=[fwd, bwd],
        out_shape=[shp, shp],
        scratch_shapes=[pltpu.VMEM((2, BRANCH_W, BRANCH_W), F32)],
        compiler_params=_cparams(("parallel", "arbitrary")),
        name="retention",
    )(rq, rk, rv, rq, rk, rv, *tabs)


def _ret_tables(ret_log_decay, cr):
    lg = -jnp.exp(ret_log_decay.astype(F32))
    idx = jnp.arange(cr, dtype=F32)
    rel = idx[:, None] - idx[None, :]
    m_f = rel >= 0
    m_b = rel < 0
    dec_f = jnp.where(m_f[None], jnp.exp(lg[0][:, None, None] * jnp.where(m_f, rel, 0.0)[None]), 0.0)
    dec_b = jnp.where(m_b[None], jnp.exp(lg[1][:, None, None] * jnp.where(m_b, -rel, 0.0)[None]), 0.0)
    dec = jnp.stack([dec_f, dec_b])
    lane_lg = jnp.repeat(lg, HEAD_W, axis=1)
    xi = jnp.stack([jnp.exp(lane_lg[0][None, :] * (idx + 1.0)[:, None]),
                    jnp.exp(lane_lg[1][None, :] * (cr - idx)[:, None])])
    zeta = jnp.stack([jnp.exp(lane_lg[0][None, :] * (cr - 1.0 - idx)[:, None]),
                      jnp.exp(lane_lg[1][None, :] * idx[:, None])])
    cd = jnp.exp(lane_lg * cr)[:, None, :]
    return dec, xi, zeta, cd


def _merge_kernel(x_ref, g_ref, sc_ref, sh_ref, g1_ref,
                  oaT_ref, ob_ref, su_ref, yf_ref, yb_ref, rf_ref, rb_ref, rg_ref,
                  s5d_ref, gluw_ref, glub_ref, rng_ref, ones_ref,
                  wg_ref, wb_ref, wo_ref, o_ref):
    tm = x_ref.shape[1]
    nh = 2 if tm % 256 == 0 else 1
    rh = tm // nh
    o_a_full = oaT_ref[0].T
    for hh in range(nh):
        rs = slice(hh * rh, (hh + 1) * rh)
        x = x_ref[0, rs, :]
        h = _norm_mod(x, g_ref[...], sc_ref[0], sh_ref[0]).astype(BF16)

        o_a = o_a_full[rs, :]
        o_b = ob_ref[0, rs, :]

        y = s5d_ref[...] * su_ref[0, rs, :] + yf_ref[0, 0, rs, :] + yb_ref[0, 0, rs, :]
        y = jax.nn.gelu(y)
        o_c = y * jax.nn.sigmoid(jnp.dot(y.astype(BF16), gluw_ref[...], preferred_element_type=F32)
                                 + glub_ref[...])

        z = rf_ref[0, rs, :] + rb_ref[0, rs, :]
        z2 = z * z
        hi = z2.astype(BF16)
        lo = (z2 - hi.astype(F32)).astype(BF16)
        ss = (jnp.dot(hi, ones_ref[...], preferred_element_type=F32)
              + jnp.dot(lo, ones_ref[...], preferred_element_type=F32))
        rg = rg_ref[0, rs, :]
        o_d = z * lax.rsqrt(ss * (1.0 / HEAD_W) + EPS) * rng_ref[...] * (rg * jax.nn.sigmoid(rg))

        merged = None
        for i, br in enumerate((o_a, o_b, o_c, o_d)):
            gate = jax.nn.sigmoid(jnp.dot(h, wg_ref[i], preferred_element_type=F32))
            t = gate * jnp.dot(br.astype(BF16), wb_ref[i], preferred_element_type=F32)
            merged = t if merged is None else merged + t
        out = jnp.dot(merged.astype(BF16), wo_ref[...], preferred_element_type=F32)
        o_ref[0, rs, :] = x + g1_ref[0] * out


def _merge_call(geo, x, g, sc, sh, g1, oaT, ob, su, y, rf, rb, rg, mw):
    b, l, d = x.shape
    tm = geo.tm
    tok = pl.BlockSpec((1, tm, BRANCH_W), lambda bb, i: (bb, i, 0))
    xs = pl.BlockSpec((1, tm, d), lambda bb, i: (bb, i, 0))
    vec = pl.BlockSpec((1, 1, d), lambda bb, i: (bb, 0, 0))
    su_spec = pl.BlockSpec((1, tm, BRANCH_W), lambda bb, i: geo.scan_index(bb, i))

    def y_spec(dd):
        return pl.BlockSpec((1, 1, tm, BRANCH_W), lambda bb, i: (dd,) + geo.scan_index(bb, i))

    s5_d, glu_w, glu_b, ret_g, ones_bd, wg, wb, wo = mw
    return pl.pallas_call(
        _merge_kernel,
        grid=(b, geo.nt),
        in_specs=[xs, _const_spec((1, d)), vec, vec, vec,
                  pl.BlockSpec((1, BRANCH_W, tm), lambda bb, i: (bb, 0, i)), tok,
                  su_spec, y_spec(0), y_spec(1), tok, tok, tok,
                  _const_spec(s5_d.shape), _const_spec(glu_w.shape), _const_spec(glu_b.shape),
                  _const_spec(ret_g.shape), _const_spec(ones_bd.shape),
                  _const_spec(wg.shape), _const_spec(wb.shape), _const_spec(wo.shape)],
        out_specs=xs,
        out_shape=jax.ShapeDtypeStruct(x.shape, F32),
        compiler_params=_cparams(("parallel", "parallel")),
        name="merge",
    )(x, g, sc, sh, g1, oaT, ob, su, y, y, rf, rb, rg, s5_d, glu_w, glu_b, ret_g, ones_bd, wg, wb, wo)


HALO = 8


def _ffn_kernel(xp_ref, x_ref, xn_ref, g_ref, sc_ref, sh_ref, g2_ref,
                wu_ref, wgt_ref, cw_ref, cb_ref, wd_ref, fg_ref, o_ref, a_scr, *, nt, final):
    i = pl.program_id(1)
    tm = x_ref.shape[1]
    g, sc, sh = g_ref[...], sc_ref[0], sh_ref[0]
    x = x_ref[0]
    h = _norm_mod(x, g, sc, sh)
    hp = _norm_mod(xp_ref[0], g, sc, sh) * (i > 0).astype(F32)
    hn = _norm_mod(xn_ref[0], g, sc, sh) * (i < nt - 1).astype(F32)
    h_ext = jnp.concatenate([hp, h, hn], axis=0).astype(BF16)
    a_scr[...] = jnp.dot(h_ext, wu_ref[...], preferred_element_type=F32)
    cw = cw_ref[...]
    a = (a_scr[pl.ds(HALO - 1, tm), :] * cw[0:1] + a_scr[pl.ds(HALO, tm), :] * cw[1:2]
         + a_scr[pl.ds(HALO + 1, tm), :] * cw[2:3] + cb_ref[...])
    gate = jnp.dot(h.astype(BF16), wgt_ref[...], preferred_element_type=F32)
    act = (jax.nn.gelu(a) * gate).astype(BF16)
    y = x + g2_ref[0] * jnp.dot(act, wd_ref[...], preferred_element_type=F32)
    if final:
        ms = jnp.mean(y * y, axis=-1, keepdims=True)
        y = y * lax.rsqrt(ms + EPS) * fg_ref[...]
    o_ref[0] = y


def _ffn_call(geo, x, g, sc, sh, g2, fw, final_g, final):
    b, l, d = x.shape
    tm = min(geo.tm, 512)
    nt = l // tm
    r = tm // HALO
    nh = l // HALO
    xs = pl.BlockSpec((1, tm, d), lambda bb, i: (bb, i, 0))
    prev = pl.BlockSpec((1, HALO, d), lambda bb, i: (bb, jnp.maximum(i * r - 1, 0), 0))
    nxt = pl.BlockSpec((1, HALO, d), lambda bb, i: (bb, jnp.minimum((i + 1) * r, nh - 1), 0))
    vec = pl.BlockSpec((1, 1, d), lambda bb, i: (bb, 0, 0))
    wu, wgt, cw, cb, wd = fw
    once = pl.Buffered(1)

    def wspec(shape):
        return pl.BlockSpec(shape, lambda *_: (0,) * len(shape), pipeline_mode=once)

    return pl.pallas_call(
        functools.partial(_ffn_kernel, nt=nt, final=final),
        grid=(b, nt),
        in_specs=[prev, xs, nxt, _const_spec((1, d)), vec, vec, vec,
                  wspec(wu.shape), wspec(wgt.shape), _const_spec(cw.shape), _const_spec(cb.shape),
                  wspec(wd.shape), _const_spec((1, d))],
        out_specs=xs,
        out_shape=jax.ShapeDtypeStruct(x.shape, F32),
        scratch_shapes=[pltpu.VMEM((tm + 2 * HALO, D_FF), F32)],
        compiler_params=_cparams(("parallel", "parallel")),
        name="conv_ffn",
    )(x, x, x, g, sc, sh, g2, wu, wgt, cw, cb, wd, final_g)


RET_CHUNK = 256


def _layer_weights(l, p):
    d = p['w_in'].shape[1]
    ones_bd = jnp.asarray(np.kron(np.eye(N_HEADS), np.ones((HEAD_W, HEAD_W))), BF16)
    return dict(
        norm1_g=p['norm1_g'][l].reshape(1, d),
        norm2_g=p['norm2_g'][l].reshape(1, d),
        w_in=p['w_in'][l].astype(BF16),
        da_lambda=p['da_lambda'][l].astype(F32),
        da_subln_g=p['da_subln_g'][l].astype(F32),
        na_bias=_na_bias_table(p['na_rpb'][l]),
        gsum=jnp.asarray(np.kron(np.eye(2 * N_HEADS), np.ones((DA_DIM, DA_DIM))), BF16),
        s5=_s5_weights(p['s5_a_re'][l], p['s5_a_im'][l], p['s5_log_dt'][l], p['s5_b_re'][l],
                       p['s5_b_im'][l], p['s5_c_re'][l], p['s5_c_im'][l]),
        ret=_ret_tables(p['ret_log_decay'][l], RET_CHUNK),
        merge=(p['s5_d'][l].reshape(1, BRANCH_W).astype(F32),
               p['s5_glu_w'][l].astype(BF16),
               p['s5_glu_b'][l].reshape(1, BRANCH_W).astype(F32),
               jnp.tile(p['ret_norm_g'][l].astype(F32), N_HEADS).reshape(1, BRANCH_W),
               ones_bd,
               p['w_branch_gate'][l].astype(BF16),
               p['w_branch'][l].astype(BF16),
               p['w_out'][l].astype(BF16)),
        ffn=(p['ffn_w_up'][l].astype(BF16),
             p['ffn_w_gate'][l].astype(BF16),
             p['ffn_conv_w'][l].reshape(3, D_FF).astype(F32),
             p['ffn_conv_b'][l].reshape(1, D_FF).astype(F32),
             p['ffn_w_down'][l].astype(BF16)),
    )


def _layer(geo, x, mod, lw, lam_init, rope, final_g, final):
    b, l, d = x.shape
    sh1, sc1, g1, sh2, sc2, g2 = [mod[:, k * d:(k + 1) * d].reshape(b, 1, d) for k in range(6)]
    (qT, k, kn, vT, nq, nk, nv, su, rq, rk, rv, rg) = _proj_call(
        geo, x, lw['norm1_g'], sc1, sh1, lw['w_in'], rope, lw['gsum'])
    oaT = _da_call(qT, k, vT, kn, lw['da_lambda'], lw['da_subln_g'], lam_init)
    ob = _na_call(geo, nq, nk, nv, lw['na_bias'])
    y = _s5_mixer(geo, su, lw['s5'])
    rf, rb = _ret_call(rq, rk, rv, lw['ret'])
    x = _merge_call(geo, x, lw['norm1_g'], sc1, sh1, g1, oaT, ob, su, y, rf, rb, rg, lw['merge'])
    return _ffn_call(geo, x, lw['norm2_g'], sc2, sh2, g2, lw['ffn'], final_g, final)


def kernel(x_prompt, x_sample, c_prompt, c_sample, norm1_g, norm2_g, ada_w, ada_b, w_in, da_lambda, da_subln_g, na_rpb, s5_a_re, s5_a_im, s5_log_dt, s5_b_re, s5_b_im, s5_c_re, s5_c_im, s5_d, s5_glu_w, s5_glu_b, ret_log_decay, ret_norm_g, w_branch, w_branch_gate, w_out, ffn_w_up, ffn_w_gate, ffn_conv_w, ffn_conv_b, ffn_w_down, final_norm_g):
    p = dict(norm1_g=norm1_g, norm2_g=norm2_g, w_in=w_in, da_lambda=da_lambda,
             da_subln_g=da_subln_g, na_rpb=na_rpb, s5_a_re=s5_a_re, s5_a_im=s5_a_im,
             s5_log_dt=s5_log_dt, s5_b_re=s5_b_re, s5_b_im=s5_b_im, s5_c_re=s5_c_re,
             s5_c_im=s5_c_im, s5_d=s5_d, s5_glu_w=s5_glu_w, s5_glu_b=s5_glu_b,
             ret_log_decay=ret_log_decay, ret_norm_g=ret_norm_g, w_branch=w_branch,
             w_branch_gate=w_branch_gate, w_out=w_out, ffn_w_up=ffn_w_up, ffn_w_gate=ffn_w_gate,
             ffn_conv_w=ffn_conv_w, ffn_conv_b=ffn_conv_b, ffn_w_down=ffn_w_down)
    depth = w_in.shape[0]
    d = w_in.shape[1]
    nbp = c_prompt.shape[0]
    mod = _mod_call(jnp.concatenate([c_prompt, c_sample], axis=0), ada_w, ada_b)
    lws = [_layer_weights(l, p) for l in range(depth)]
    final_g = final_norm_g.reshape(1, d).astype(F32)
    outs = []
    for x, lo in ((x_prompt, 0), (x_sample, nbp)):
        b, l, _ = x.shape
        geo = _Geom(b, l)
        rope = _rope_tables(l)
        for li in range(depth):
            lam_init = 0.8 - 0.6 * math.exp(-0.3 * li)
            x = _layer(geo, x, mod[li, lo:lo + b], lws[li], lam_init, rope, final_g,
                       final=(li == depth - 1))
        outs.append(x)
    return tuple(outs)
```
